```python
import math
import jax, jax.numpy as jnp
from jax import lax
import numpy as np

D_MODEL = 2048
BATCH = 4
SEQ = 2048
DEPTH = 2
DEC_BATCH = 128
DEC_SEQ = 4
PAST_LEN = 2048
PAGE_SIZE = 128

HEAD_DIM = 128
W_SB = D_MODEL // 2
H_SB = W_SB // HEAD_DIM
W_SSM = D_MODEL - W_SB
SSM_GROUP = 16
G_SSM = W_SSM // SSM_GROUP
P_SSM = 64
W_FOX = D_MODEL
H_FOX = W_FOX // HEAD_DIM
Q_BLOCK = 128
N_EVEN = (DEPTH + 1) // 2
N_ODD = DEPTH // 2
IN_EVEN = 4 * W_SB + 2 * W_SSM
IN_ODD = 4 * W_FOX + H_FOX
RMS_EPS = 1e-6
DT_MIN = 1e-3
DT_MAX = 1e-1

kernel_name = "hybrid_sb_s5_fox_decode_step"


def rms_norm(x, g):
    xf = x.astype(jnp.float32)
    y = xf * lax.rsqrt(jnp.mean(xf * xf, axis=-1, keepdims=True) + RMS_EPS)
    return (y * g.astype(jnp.float32)).astype(x.dtype)


def stick_breaking_attention(q, k, v):
    lq, lk = q.shape[1], k.shape[1]
    off = lk - lq
    scale = HEAD_DIM ** -0.5
    outs = []
    for i0 in range(0, lq, Q_BLOCK):
        i1 = min(i0 + Q_BLOCK, lq)
        kend = off + i1
        z = jnp.einsum('bqhd,bkhd->bhqk', q[:, i0:i1], k[:, :kend],
                       preferred_element_type=jnp.float32) * scale
        t_pos = off + jnp.arange(i0, i1)
        s_pos = jnp.arange(kend)
        before = s_pos[None, :] < t_pos[:, None]
        log_fail = jnp.where(before, jax.nn.log_sigmoid(-z), 0.0)
        between = lax.cumsum(log_fail, axis=3, reverse=True) - log_fail
        w = jnp.where(before, jnp.exp(jax.nn.log_sigmoid(z) + between), 0.0)
        outs.append(jnp.einsum('bhqk,bkhd->bqhd', w.astype(v.dtype), v[:, :kend]))
    return jnp.concatenate(outs, axis=1)


def forgetting_attention(q, k, v, cum_logf):
    lq, lk = q.shape[1], k.shape[1]
    off = lk - lq
    scale = HEAD_DIM ** -0.5
    c = jnp.transpose(cum_logf, (0, 2, 1))
    outs = []
    for i0 in range(0, lq, Q_BLOCK):
        i1 = min(i0 + Q_BLOCK, lq)
        kend = off + i1
        z = jnp.einsum('bqhd,bkhd->bhqk', q[:, i0:i1], k[:, :kend],
                       preferred_element_type=jnp.float32) * scale
        bias = c[:, :, off + i0:off + i1, None] - c[:, :, None, :kend]
        t_pos = off + jnp.arange(i0, i1)
        s_pos = jnp.arange(kend)
        causal = s_pos[None, :] <= t_pos[:, None]
        p = jax.nn.softmax(jnp.where(causal, z + bias, -jnp.inf), axis=-1)
        outs.append(jnp.einsum('bhqk,bkhd->bqhd', p.astype(v.dtype), v[:, :kend]))
    return jnp.concatenate(outs, axis=1)


def _complex_affine_combine(e1, e2):
    a1r, a1i, b1r, b1i = e1
    a2r, a2i, b2r, b2i = e2
    ar = a2r * a1r - a2i * a1i
    ai = a2r * a1i + a2i * a1r
    br = a2r * b1r - a2i * b1i + b2r
    bi = a2r * b1i + a2i * b1r + b2i
    return (ar, ai, br, bi)


def s5_ssm(u, a_re, a_im, log_dt, b_re, b_im, c_re, c_im, d, h0_re, h0_im):
    bsz, L, _ = u.shape
    uf = u.astype(jnp.float32).reshape(bsz, L, G_SSM, SSM_GROUP)
    ar = a_re.astype(jnp.float32)
    ai = a_im.astype(jnp.float32)
    dt = jnp.exp(log_dt.astype(jnp.float32))[:, None]
    mag = jnp.exp(ar * dt)
    abar_re = mag * jnp.cos(ai * dt)
    abar_im = mag * jnp.sin(ai * dt)
    den = ar * ar + ai * ai
    nr = abar_re - 1.0
    ni = abar_im
    coef_re = (nr * ar + ni * ai) / den
    coef_im = (ni * ar - nr * ai) / den
    br = b_re.astype(jnp.float32)
    bi = b_im.astype(jnp.float32)
    bbar_re = coef_re[..., None] * br - coef_im[..., None] * bi
    bbar_im = coef_re[..., None] * bi + coef_im[..., None] * br
    bu_re = jnp.einsum('blgc,gpc->blgp', uf, bbar_re)
    bu_im = jnp.einsum('blgc,gpc->blgp', uf, bbar_im)
    shp = bu_re.shape
    acr, aci, hr, hi = lax.associative_scan(
        _complex_affine_combine,
        (jnp.broadcast_to(abar_re, shp), jnp.broadcast_to(abar_im, shp), bu_re, bu_im),
        axis=1)
    h0r = h0_re.astype(jnp.float32)[:, None]
    h0i = h0_im.astype(jnp.float32)[:, None]
    h_re = acr * h0r - aci * h0i + hr
    h_im = acr * h0i + aci * h0r + hi
    y = (jnp.einsum('gcp,blgp->blgc', c_re.astype(jnp.float32), h_re)
         - jnp.einsum('gcp,blgp->blgc', c_im.astype(jnp.float32), h_im)
         + d.astype(jnp.float32).reshape(G_SSM, SSM_GROUP) * uf)
    return y.reshape(bsz, L, W_SSM).astype(u.dtype), h_re[:, -1], h_im[:, -1]


def even_layer(x, g, w_in, a_re, a_im, log_dt, b_re, b_im, c_re, c_im, d, w_glu, b_glu, w_out,
               k_past, v_past, h0_re, h0_im):
    bsz, L, _ = x.shape
    h = rms_norm(x, g)
    proj = h @ w_in
    q, k, v, gate_a, u, gate_b = jnp.split(
        proj, [W_SB, 2 * W_SB, 3 * W_SB, 4 * W_SB, 4 * W_SB + W_SSM], axis=-1)
    q = q.reshape(bsz, L, H_SB, HEAD_DIM)
    k = k.reshape(bsz, L, H_SB, HEAD_DIM)
    v = v.reshape(bsz, L, H_SB, HEAD_DIM)
    k_all = jnp.concatenate([k_past.astype(k.dtype), k], axis=1)
    v_all = jnp.concatenate([v_past.astype(v.dtype), v], axis=1)
    o_a = stick_breaking_attention(q, k_all, v_all).reshape(bsz, L, W_SB) * jax.nn.silu(gate_a)
    y_b, hT_re, hT_im = s5_ssm(u, a_re, a_im, log_dt, b_re, b_im, c_re, c_im, d, h0_re, h0_im)
    y_b = jax.nn.gelu(y_b)
    y_b = y_b * jax.nn.sigmoid(y_b @ w_glu + b_glu)
    y_b = y_b * jax.nn.silu(gate_b)
    out = jnp.concatenate([o_a, y_b], axis=-1) @ w_out
    return x + out, k, v, hT_re, hT_im


def odd_layer(x, g, w_in, b_f, w_out, k_past, v_past, logf_past):
    bsz, L, _ = x.shape
    h = rms_norm(x, g)
    proj = h @ w_in
    q, k, v, gate, f_logit = jnp.split(proj, [W_FOX, 2 * W_FOX, 3 * W_FOX, 4 * W_FOX], axis=-1)
    q = q.reshape(bsz, L, H_FOX, HEAD_DIM)
    k = k.reshape(bsz, L, H_FOX, HEAD_DIM)
    v = v.reshape(bsz, L, H_FOX, HEAD_DIM)
    logf = jax.nn.log_sigmoid(f_logit.astype(jnp.float32) + b_f.astype(jnp.float32))
    k_all = jnp.concatenate([k_past.astype(k.dtype), k], axis=1)
    v_all = jnp.concatenate([v_past.astype(v.dtype), v], axis=1)
    logf_all = jnp.concatenate([logf_past.astype(jnp.float32), logf], axis=1)
    cum = lax.cumsum(logf_all, axis=1)
    o = forgetting_attention(q, k_all, v_all, cum).reshape(bsz, L, W_FOX) * jax.nn.silu(gate)
    return x + o @ w_out, k, v, logf


def setup_inputs(seed: int = 0) -> dict:
    key = jax.random.key(seed)
    ks = jax.random.split(key, 32)
    n_pages = PAST_LEN // PAGE_SIZE
    n_used = DEC_BATCH * n_pages
    n_pool = n_used + max(1, n_used // 4)
    nrm = jax.random.normal
    f32 = jnp.float32
    page_table = jax.random.permutation(ks[10], n_pool)[:n_used].reshape(DEC_BATCH, n_pages).astype(jnp.int32)
    a_im = math.pi * jnp.arange(P_SSM, dtype=f32)[None, None, :] + 0.01 * nrm(ks[14], (N_EVEN, G_SSM, P_SSM), f32)
    return {
        "x_prompt": nrm(ks[0], (BATCH, SEQ, D_MODEL), f32),
        "x_sample": nrm(ks[1], (DEC_BATCH, DEC_SEQ, D_MODEL), f32),
        "cache_sb_k": nrm(ks[2], (N_EVEN, n_pool, PAGE_SIZE, H_SB, HEAD_DIM), f32),
        "cache_sb_v": nrm(ks[3], (N_EVEN, n_pool, PAGE_SIZE, H_SB, HEAD_DIM), f32),
        "state_ssm_re": 0.1 * nrm(ks[4], (N_EVEN, DEC_BATCH, G_SSM, P_SSM), f32),
        "state_ssm_im": 0.1 * nrm(ks[5], (N_EVEN, DEC_BATCH, G_SSM, P_SSM), f32),
        "cache_fox_k": nrm(ks[6], (N_ODD, n_pool, PAGE_SIZE, H_FOX, HEAD_DIM), f32),
        "cache_fox_v": nrm(ks[7], (N_ODD, n_pool, PAGE_SIZE, H_FOX, HEAD_DIM), f32),
        "cache_fox_logf": jax.nn.log_sigmoid(2.0 + nrm(ks[8], (N_ODD, n_pool, PAGE_SIZE, H_FOX), f32)),
        "page_table": page_table,
        "norm_g": 1.0 + 0.02 * nrm(ks[11], (DEPTH, D_MODEL), f32),
        "w_in_even": nrm(ks[12], (N_EVEN, D_MODEL, IN_EVEN), f32) * D_MODEL ** -0.5,
        "ssm_a_re": -0.5 + 0.01 * nrm(ks[13], (N_EVEN, G_SSM, P_SSM), f32),
        "ssm_a_im": a_im,
        "ssm_log_dt": jax.random.uniform(ks[15], (N_EVEN, G_SSM), f32, math.log(DT_MIN), math.log(DT_MAX)),
        "ssm_b_re": nrm(ks[16], (N_EVEN, G_SSM, P_SSM, SSM_GROUP), f32) * (2 * SSM_GROUP) ** -0.5,
        "ssm_b_im": nrm(ks[17], (N_EVEN, G_SSM, P_SSM, SSM_GROUP), f32) * (2 * SSM_GROUP) ** -0.5,
        "ssm_c_re": nrm(ks[18], (N_EVEN, G_SSM, SSM_GROUP, P_SSM), f32) * (2 * P_SSM) ** -0.5,
        "ssm_c_im": nrm(ks[19], (N_EVEN, G_SSM, SSM_GROUP, P_SSM), f32) * (2 * P_SSM) ** -0.5,
        "ssm_d": nrm(ks[20], (N_EVEN, W_SSM), f32),
        "w_glu": nrm(ks[21], (N_EVEN, W_SSM, W_SSM), f32) * W_SSM ** -0.5,
        "b_glu": 0.02 * nrm(ks[22], (N_EVEN, W_SSM), f32),
        "w_out_even": nrm(ks[23], (N_EVEN, W_SB + W_SSM, D_MODEL), f32) * (W_SB + W_SSM) ** -0.5,
        "w_in_odd": nrm(ks[24], (N_ODD, D_MODEL, IN_ODD), f32) * D_MODEL ** -0.5,
        "b_f": 2.0 + 0.1 * nrm(ks[25], (N_ODD, H_FOX), f32),
        "w_out_odd": nrm(ks[26], (N_ODD, W_FOX, D_MODEL), f32) * W_FOX ** -0.5,
        "norm_f_g": 1.0 + 0.02 * nrm(ks[27], (D_MODEL,), f32),
    }


def reference(x_prompt, x_sample, cache_sb_k, cache_sb_v, state_ssm_re, state_ssm_im,
              cache_fox_k, cache_fox_v, cache_fox_logf, page_table,
              norm_g, w_in_even, ssm_a_re, ssm_a_im, ssm_log_dt, ssm_b_re, ssm_b_im,
              ssm_c_re, ssm_c_im, ssm_d, w_glu, b_glu, w_out_even,
              w_in_odd, b_f, w_out_odd, norm_f_g):
    def gather_past(pool, li):
        gp = pool[li, page_table]
        return gp.reshape((gp.shape[0], gp.shape[1] * gp.shape[2]) + gp.shape[3:])

    bp = x_prompt.shape[0]
    yp, ys = x_prompt, x_sample
    sbk_p, sbv_p, sbk_s, sbv_s = [], [], [], []
    hre_p, him_p, hre_s, him_s = [], [], [], []
    fk_p, fv_p, fl_p, fk_s, fv_s, fl_s = [], [], [], [], [], []
    for layer in range(DEPTH):
        g = norm_g[layer]
        if layer % 2 == 0:
            i = layer // 2
            prm = (w_in_even[i], ssm_a_re[i], ssm_a_im[i], ssm_log_dt[i], ssm_b_re[i], ssm_b_im[i],
                   ssm_c_re[i], ssm_c_im[i], ssm_d[i], w_glu[i], b_glu[i], w_out_even[i])
            empty_kv = jnp.zeros((bp, 0, H_SB, HEAD_DIM), x_prompt.dtype)
            zero_h = jnp.zeros((bp, G_SSM, P_SSM), jnp.float32)
            yp, k1, v1, r1, m1 = even_layer(yp, g, *prm, empty_kv, empty_kv, zero_h, zero_h)
            ys, k2, v2, r2, m2 = even_layer(ys, g, *prm, gather_past(cache_sb_k, i), gather_past(cache_sb_v, i),
                                            state_ssm_re[i], state_ssm_im[i])
            sbk_p.append(k1); sbv_p.append(v1); hre_p.append(r1); him_p.append(m1)
            sbk_s.append(k2); sbv_s.append(v2); hre_s.append(r2); him_s.append(m2)
        else:
            i = layer // 2
            empty_kv = jnp.zeros((bp, 0, H_FOX, HEAD_DIM), x_prompt.dtype)
            empty_lf = jnp.zeros((bp, 0, H_FOX), jnp.float32)
            yp, k1, v1, l1 = odd_layer(yp, g, w_in_odd[i], b_f[i], w_out_odd[i], empty_kv, empty_kv, empty_lf)
            ys, k2, v2, l2 = odd_layer(ys, g, w_in_odd[i], b_f[i], w_out_odd[i],
                                       gather_past(cache_fox_k, i), gather_past(cache_fox_v, i),
                                       gather_past(cache_fox_logf, i))
            fk_p.append(k1); fv_p.append(v1); fl_p.append(l1)
            fk_s.append(k2); fv_s.append(v2); fl_s.append(l2)
    y_prompt = rms_norm(yp, norm_f_g)
    y_sample = rms_norm(ys, norm_f_g)
    sb_k_prompt = jnp.stack(sbk_p)
    sb_v_prompt = jnp.stack(sbv_p)
    sb_k_sample = jnp.stack(sbk_s)
    sb_v_sample = jnp.stack(sbv_s)
    ssm_re_prompt = jnp.stack(hre_p)
    ssm_im_prompt = jnp.stack(him_p)
    ssm_re_sample = jnp.stack(hre_s)
    ssm_im_sample = jnp.stack(him_s)
    fox_k_prompt = jnp.stack(fk_p)
    fox_v_prompt = jnp.stack(fv_p)
    fox_logf_prompt = jnp.stack(fl_p)
    fox_k_sample = jnp.stack(fk_s)
    fox_v_sample = jnp.stack(fv_s)
    fox_logf_sample = jnp.stack(fl_s)
    return (y_prompt, y_sample, sb_k_prompt, sb_v_prompt, sb_k_sample, sb_v_sample,
            ssm_re_prompt, ssm_im_prompt, ssm_re_sample, ssm_im_sample,
            fox_k_prompt, fox_v_prompt, fox_logf_prompt, fox_k_sample, fox_v_sample, fox_logf_sample)
```

```python
import functools

import jax
import jax.numpy as jnp
from jax import lax
from jax.experimental import pallas as pl
from jax.experimental.pallas import tpu as pltpu

F32 = jnp.float32
BF16 = jnp.bfloat16

HEAD_DIM = 128
SSM_GROUP = 16
P_SSM = 64
RMS_EPS = 1e-6
PAGE = 128
PAGES_PER_STEP = 4
SCAN_ROWS = 8
NEG_BIG = -1e30
MIB = 1024 * 1024


def _cparams(sem, vmem_mib):
    return pltpu.CompilerParams(dimension_semantics=sem, vmem_limit_bytes=vmem_mib * MIB)


def _split2(x):
    hi = x.astype(BF16)
    lo = (x - hi.astype(F32)).astype(BF16)
    return hi, lo


def _split3(x):
    hi = x.astype(BF16)
    r = x - hi.astype(F32)
    mid = r.astype(BF16)
    lo = (r - mid.astype(F32)).astype(BF16)
    return hi, mid, lo


def _dot(a, b):
    return jnp.dot(a, b, preferred_element_type=F32)


def _dot_nt(a, b):
    return lax.dot_general(a, b, (((1,), (1,)), ((), ())), preferred_element_type=F32)


def _neg_softplus(z):
    return -(jnp.maximum(z, 0.0) + jnp.log1p(jnp.exp(-jnp.abs(z))))


def _tri(n, upper):
    r = lax.broadcasted_iota(jnp.int32, (n, n), 0)
    c = lax.broadcasted_iota(jnp.int32, (n, n), 1)
    return ((c > r) if upper else (r > c)).astype(BF16)


def _norm_proj_kernel(x_ref, g_ref, w_ref, o_ref, h_scr):
    @pl.when(pl.program_id(1) == 0)
    def _():
        x = x_ref[...]
        y = x * lax.rsqrt(jnp.mean(x * x, axis=-1, keepdims=True) + RMS_EPS)
        h_scr[...] = (y * g_ref[...]).astype(BF16)

    o_ref[...] = _dot(h_scr[...], w_ref[...])


def norm_proj(x, g, w, tm, tn):
    m, d = x.shape
    n = w.shape[1]
    return pl.pallas_call(
        _norm_proj_kernel,
        grid=(m // tm, n // tn),
        in_specs=[
            pl.BlockSpec((tm, d), lambda i, j: (i, 0)),
            pl.BlockSpec((1, d), lambda i, j: (0, 0)),
            pl.BlockSpec((d, tn), lambda i, j: (0, j)),
        ],
        out_specs=pl.BlockSpec((tm, tn), lambda i, j: (i, j)),
        out_shape=jax.ShapeDtypeStruct((m, n), F32),
        scratch_shapes=[pltpu.VMEM((tm, d), BF16)],
        compiler_params=_cparams(("parallel", "arbitrary"), 48),
        name="norm_proj",
    )(x, g.reshape(1, d), w)


def _logf_kernel(x_ref, g_ref, w_ref, b_ref, o_ref):
    x = x_ref[...]
    y = x * lax.rsqrt(jnp.mean(x * x, axis=-1, keepdims=True) + RMS_EPS)
    h = (y * g_ref[...]).astype(BF16)
    f = _dot(h, w_ref[...]) + b_ref[...]
    o_ref[...] = _neg_softplus(-f)


def fox_logf(x, g, w_pad, b_pad, tm):
    m, d = x.shape
    n = w_pad.shape[1]
    return pl.pallas_call(
        _logf_kernel,
        grid=(m // tm,),
        in_specs=[
            pl.BlockSpec((tm, d), lambda i: (i, 0)),
            pl.BlockSpec((1, d), lambda i: (0, 0)),
            pl.BlockSpec((d, n), lambda i: (0, 0)),
            pl.BlockSpec((1, n), lambda i: (0, 0)),
        ],
        out_specs=pl.BlockSpec((tm, n), lambda i: (i, 0)),
        out_shape=jax.ShapeDtypeStruct((m, n), F32),
        compiler_params=_cparams(("parallel",), 32),
        name="fox_logf",
    )(x, g.reshape(1, d), w_pad, b_pad)


def _cumsum_kernel(x_ref, o_ref, *, blk):
    n = x_ref.shape[0]
    r = lax.broadcasted_iota(jnp.int32, (blk, blk), 0)
    c = lax.broadcasted_iota(jnp.int32, (blk, blk), 1)
    low = (c <= r).astype(BF16)

    def body(i, car):
        off = pl.multiple_of(i * blk, blk)
        hi, mid, lo = _split3(x_ref[pl.ds(off, blk), :])
        cum = _dot(low, hi) + _dot(low, mid) + _dot(low, lo) + car
        o_ref[pl.ds(off, blk), :] = cum
        return cum[blk - 1:blk, :]

    lax.fori_loop(0, n // blk, body, jnp.zeros((1, x_ref.shape[1]), F32))


def seq_cumsum(x):
    b, n, w = x.shape
    return pl.pallas_call(
        functools.partial(_cumsum_kernel, blk=128),
        grid=(b,),
        in_specs=[pl.BlockSpec((None, n, w), lambda i: (i, 0, 0))],
        out_specs=pl.BlockSpec((None, n, w), lambda i: (i, 0, 0)),
        out_shape=jax.ShapeDtypeStruct((b, n, w), F32),
        compiler_params=_cparams(("parallel",), 32),
        name="seq_cumsum",
    )(x)


def _sb_prompt_kernel(q_ref, k_ref, v_ref, o_ref, *, tq, tk):
    i = pl.program_id(2)
    q = (q_ref[...] * (HEAD_DIM ** -0.5)).astype(BF16)
    t_pos = i * tq + lax.broadcasted_iota(jnp.int32, (tq, tk), 0)
    s_loc = lax.broadcasted_iota(jnp.int32, (tq, tk), 1)
    later = _tri(tk, upper=False)
    nblk = (i + 1) * (tq // tk)

    def body(n, carry):
        acc, car = carry
        off = pl.multiple_of((nblk - 1 - n) * tk, tk)
        k = k_ref[pl.ds(off, tk), :].astype(BF16)
        v = v_ref[pl.ds(off, tk), :].astype(BF16)
        z = _dot_nt(q, k)
        before = (off + s_loc) < t_pos
        lf = jnp.where(before, _neg_softplus(z), 0.0)
        hi, lo = _split2(lf)
        between = _dot(hi, later) + _dot(lo, later) + car
        w = jnp.where(before, jnp.exp(z + lf + between), 0.0)
        acc = acc + _dot(w.astype(BF16), v)
        car = car + jnp.sum(lf, axis=1, keepdims=True)
        return acc, car

    acc, _ = lax.fori_loop(0, nblk, body,
                           (jnp.zeros((tq, HEAD_DIM), F32), jnp.zeros((tq, 1), F32)))
    o_ref[...] = acc


def sb_attn_prompt(proj, nb, seq, nh, tq=256, tk=128):
    nq = seq // tq
    return pl.pallas_call(
        functools.partial(_sb_prompt_kernel, tq=tq, tk=tk),
        grid=(nb, nh, nq),
        in_specs=[
            pl.BlockSpec((tq, HEAD_DIM), lambda b, h, i: (b * nq + i, h)),
            pl.BlockSpec((seq, HEAD_DIM), lambda b, h, i: (b, nh + h)),
            pl.BlockSpec((seq, HEAD_DIM), lambda b, h, i: (b, 2 * nh + h)),
        ],
        out_specs=pl.BlockSpec((tq, HEAD_DIM), lambda b, h, i: (b * nq + i, h)),
        out_shape=jax.ShapeDtypeStruct((nb * seq, nh * HEAD_DIM), F32),
        compiler_params=_cparams(("parallel", "parallel", "arbitrary"), 32),
        name="sb_attn_prompt",
    )(proj, proj, proj)


def _fox_prompt_kernel(q_ref, k_ref, v_ref, ccol_ref, crow_ref, o_ref, *, tq, tk):
    i = pl.program_id(2)
    q = (q_ref[...] * (HEAD_DIM ** -0.5)).astype(BF16)
    ct = ccol_ref[...]
    t_pos = i * tq + lax.broadcasted_iota(jnp.int32, (tq, tk), 0)
    s_loc = lax.broadcasted_iota(jnp.int32, (tq, tk), 1)
    nblk = (i + 1) * (tq // tk)

    def body(j, carry):
        m, l, acc = carry
        off = pl.multiple_of(j * tk, tk)
        k = k_ref[pl.ds(off, tk), :].astype(BF16)
        v = v_ref[pl.ds(off, tk), :].astype(BF16)
        cs = crow_ref[:, pl.ds(off, tk)]
        s = _dot_nt(q, k) + (ct - cs)
        s = jnp.where((off + s_loc) <= t_pos, s, NEG_BIG)
        m_new = jnp.maximum(m, jnp.max(s, axis=1, keepdims=True))
        alpha = jnp.exp(m - m_new)
        p = jnp.exp(s - m_new)
        l = alpha * l + jnp.sum(p, axis=1, keepdims=True)
        acc = alpha * acc + _dot(p.astype(BF16), v)
        return m_new, l, acc

    m, l, acc = lax.fori_loop(
        0, nblk, body,
        (jnp.full((tq, 1), NEG_BIG, F32), jnp.zeros((tq, 1), F32), jnp.zeros((tq, HEAD_DIM), F32)))
    o_ref[...] = acc / l


def fox_attn_prompt(proj, cum_col, cum_row, nb, seq, nh, tq=256, tk=256):
    nq = seq // tq
    return pl.pallas_call(
        functools.partial(_fox_prompt_kernel, tq=tq, tk=tk),
        grid=(nb, nh, nq),
        in_specs=[
            pl.BlockSpec((tq, HEAD_DIM), lambda b, h, i: (b * nq + i, h)),
            pl.BlockSpec((seq, HEAD_DIM), lambda b, h, i: (b, nh + h)),
            pl.BlockSpec((seq, HEAD_DIM), lambda b, h, i: (b, 2 * nh + h)),
            pl.BlockSpec((None, None, tq, 1), lambda b, h, i: (b, h, i, 0)),
            pl.BlockSpec((None, None, 1, seq), lambda b, h, i: (b, h, 0, 0)),
        ],
        out_specs=pl.BlockSpec((tq, HEAD_DIM), lambda b, h, i: (b * nq + i, h)),
        out_shape=jax.ShapeDtypeStruct((nb * seq, nh * HEAD_DIM), F32),
        compiler_params=_cparams(("parallel", "parallel", "arbitrary"), 32),
        name="fox_attn_prompt",
    )(proj, proj, proj, cum_col, cum_row)


def _block_diag_queries(q8, nh, nq):
    w = q8.shape[1]
    head_of_lane = lax.broadcasted_iota(jnp.int32, (nh, w), 1) // HEAD_DIM
    head_of_row = lax.broadcasted_iota(jnp.int32, (nh, w), 0)
    sel = head_of_lane == head_of_row
    rows = [jnp.where(sel, jnp.broadcast_to(q8[t:t + 1, :], (nh, w)), 0.0) for t in range(nq)]
    rows.append(jnp.zeros((PAGE - nq * nh, w), F32))
    return jnp.concatenate(rows, axis=0).astype(BF16)


def _gather_heads(acc, nh, nq):
    w = acc.shape[1]
    head_of_lane = lax.broadcasted_iota(jnp.int32, (nh, w), 1) // HEAD_DIM
    head_of_row = lax.broadcasted_iota(jnp.int32, (nh, w), 0)
    sel = head_of_lane == head_of_row
    rows = [jnp.sum(jnp.where(sel, acc[t * nh:(t + 1) * nh, :], 0.0), axis=0, keepdims=True)
            for t in range(nq)]
    rows.append(jnp.zeros((8 - nq, w), F32))
    return jnp.concatenate(rows, axis=0)


def _new_rows_page(rows8):
    return jnp.concatenate([rows8, jnp.zeros((PAGE - 8, rows8.shape[1]), F32)], axis=0).astype(BF16)


def _sb_decode_kernel(pt_ref, q_ref, kn_ref, vn_ref, *refs, nh, nq, n_chunks):
    pps = PAGES_PER_STEP
    k_refs = refs[:pps]
    v_refs = refs[pps:2 * pps]
    o_ref = refs[2 * pps]
    qbd_scr, w_scr, wn_scr, car_scr, acc_scr = refs[2 * pps + 1:]
    s = pl.program_id(1)
    nl = nq * nh
    later = _tri(PAGE, upper=True)

    def weights(kpage, car, vis):
        z = _dot_nt(kpage, qbd_scr[...])
        lf = _neg_softplus(z)
        if vis is not None:
            lf = jnp.where(vis, lf, 0.0)
        hi, lo = _split2(lf)
        between = _dot(later, hi) + _dot(later, lo) + car
        w = jnp.exp(z + lf + between)
        if vis is not None:
            w = jnp.where(vis, w, 0.0)
        return w, car + jnp.sum(lf, axis=0, keepdims=True)

    @pl.when(s == 0)
    def _():
        qbd_scr[...] = _block_diag_queries(q_ref[...] * (HEAD_DIM ** -0.5), nh, nq)
        key = lax.broadcasted_iota(jnp.int32, (PAGE, PAGE), 0)
        qry = lax.broadcasted_iota(jnp.int32, (PAGE, PAGE), 1) // nh
        w, car = weights(_new_rows_page(kn_ref[...]), jnp.zeros((1, PAGE), F32), key < qry)
        wn_scr[...] = w
        car_scr[...] = car

    @pl.when(s < n_chunks)
    def _():
        car = car_scr[...]
        for r in range(pps):
            w, car = weights(k_refs[r][...].astype(BF16), car, None)
            row = pl.multiple_of((s * pps + r) * PAGE, PAGE)
            w_scr[pl.ds(row, PAGE), :] = w
        car_scr[...] = car

    @pl.when(s == n_chunks)
    def _():
        wt = wn_scr[...].T[:nl, :].astype(BF16)
        acc_scr[...] = _dot(wt, _new_rows_page(vn_ref[...]))

    @pl.when(s >= n_chunks)
    def _():
        acc = acc_scr[...]
        for r in range(pps):
            row = pl.multiple_of(((s - n_chunks) * pps + r) * PAGE, PAGE)
            wt = w_scr[pl.ds(row, PAGE), :].T[:nl, :].astype(BF16)
            acc = acc + _dot(wt, v_refs[r][...].astype(BF16))
        acc_scr[...] = acc

    @pl.when(s == 2 * n_chunks - 1)
    def _():
        o_ref[...] = _gather_heads(acc_scr[...], nh, nq)


def _page_specs(width, n_pages, n_chunks, phase):
    specs = []
    for r in range(PAGES_PER_STEP):
        def imap(b, s, pt, r=r):
            if phase == 0:
                c = jnp.minimum(s, n_chunks - 1)
            else:
                c = jnp.maximum(s - n_chunks, 0)
            return (pt[b, n_pages - 1 - (c * PAGES_PER_STEP + r)], 0, 0)
        specs.append(pl.BlockSpec((None, PAGE, width), imap))
    return specs


def sb_attn_decode(page_table, q8, kn8, vn8, pool_k, pool_v, nh, nq):
    nb, n_pages = page_table.shape
    w = nh * HEAD_DIM
    n_chunks = n_pages // PAGES_PER_STEP
    row_spec = pl.BlockSpec((None, 8, w), lambda b, s, pt: (b, 0, 0))
    grid_spec = pltpu.PrefetchScalarGridSpec(
        num_scalar_prefetch=1,
        grid=(nb, 2 * n_chunks),
        in_specs=[row_spec, row_spec, row_spec]
        + _page_specs(w, n_pages, n_chunks, 0) + _page_specs(w, n_pages, n_chunks, 1),
        out_specs=row_spec,
        scratch_shapes=[
            pltpu.VMEM((PAGE, w), BF16),
            pltpu.VMEM((n_pages * PAGE, PAGE), F32),
            pltpu.VMEM((PAGE, PAGE), F32),
            pltpu.VMEM((1, PAGE), F32),
            pltpu.VMEM((nq * nh, w), F32),
        ],
    )
    return pl.pallas_call(
        functools.partial(_sb_decode_kernel, nh=nh, nq=nq, n_chunks=n_chunks),
        grid_spec=grid_spec,
        out_shape=jax.ShapeDtypeStruct((nb, 8, w), F32),
        compiler_params=_cparams(("arbitrary", "arbitrary"), 48),
        name="sb_attn_decode",
    )(page_table, q8, kn8, vn8, *([pool_k] * PAGES_PER_STEP), *([pool_v] * PAGES_PER_STEP))


def _tile_lanes(x16, nh, nq):
    out = x16
    for t in range(1, nq):
        out = out + pltpu.roll(x16, t * nh, axis=1)
    return out


def _fox_decode_kernel(pt_ref, q_ref, kn_ref, vn_ref, lfn_ref, *refs, nh, nq, n_chunks):
    pps = PAGES_PER_STEP
    k_refs = refs[:pps]
    v_refs = refs[pps:2 * pps]
    lf_refs = refs[2 * pps:3 * pps]
    o_ref = refs[3 * pps]
    qbd_scr, s_scr, sn_scr, lfpad_scr, car_scr, pre_scr, m_scr, l_scr, acc_scr = refs[3 * pps + 1:]
    s = pl.program_id(1)
    nl = nq * nh
    later = _tri(PAGE, upper=True)

    @pl.when(s == 0)
    def _():
        qbd_scr[...] = _block_diag_queries(q_ref[...] * (HEAD_DIM ** -0.5), nh, nq)
        lfpad_scr[...] = jnp.zeros((PAGE, PAGE), F32)
        lfn = lfn_ref[...]
        key8 = lax.broadcasted_iota(jnp.int32, (8, PAGE), 0)
        qry8 = lax.broadcasted_iota(jnp.int32, (8, PAGE), 1) // nh
        pre_q = jnp.sum(jnp.where(key8 <= qry8, lfn, 0.0), axis=0, keepdims=True)
        run = jnp.zeros((1, PAGE), F32)
        pre_rows = []
        for j in range(8):
            run = run + lfn[j:j + 1, :]
            pre_rows.append(run)
        pre_k = jnp.concatenate(pre_rows, axis=0)
        z = _dot_nt(_new_rows_page(kn_ref[...]), qbd_scr[...])[:8, :]
        sc = jnp.where(key8 <= qry8, z + (pre_q - pre_k), NEG_BIG)
        sn_scr[...] = sc
        pre_scr[...] = pre_q
        car_scr[...] = jnp.zeros((1, PAGE), F32)
        m_scr[...] = jnp.max(sc, axis=0, keepdims=True)

    @pl.when(s < n_chunks)
    def _():
        car = car_scr[...]
        m = m_scr[...]
        pre_q = pre_scr[...]
        for r in range(pps):
            lfpad_scr[:, 0:nh] = lf_refs[r][...]
            lf = _tile_lanes(lfpad_scr[...], nh, nq)
            hi, mid, lo = _split3(lf)
            after = _dot(later, hi) + _dot(later, mid) + _dot(later, lo) + car
            sc = _dot_nt(k_refs[r][...].astype(BF16), qbd_scr[...]) + (after + pre_q)
            row = pl.multiple_of((s * pps + r) * PAGE, PAGE)
            s_scr[pl.ds(row, PAGE), :] = sc
            m = jnp.maximum(m, jnp.max(sc, axis=0, keepdims=True))
            car = car + jnp.sum(lf, axis=0, keepdims=True)
        car_scr[...] = car
        m_scr[...] = m

    @pl.when(s == n_chunks)
    def _():
        p = jnp.exp(sn_scr[...] - m_scr[...])
        l_scr[...] = jnp.sum(p, axis=0, keepdims=True)
        pt = jnp.concatenate([p, jnp.zeros((PAGE - 8, PAGE), F32)], axis=0).T[:nl, :].astype(BF16)
        acc_scr[...] = _dot(pt, _new_rows_page(vn_ref[...]))

    @pl.when(s >= n_chunks)
    def _():
        acc = acc_scr[...]
        l = l_scr[...]
        m = m_scr[...]
        for r in range(pps):
            row = pl.multiple_of(((s - n_chunks) * pps + r) * PAGE, PAGE)
            p = jnp.exp(s_scr[pl.ds(row, PAGE), :] - m)
            l = l + jnp.sum(p, axis=0, keepdims=True)
            acc = acc + _dot(p.T[:nl, :].astype(BF16), v_refs[r][...].astype(BF16))
        acc_scr[...] = acc
        l_scr[...] = l

    @pl.when(s == 2 * n_chunks - 1)
    def _():
        l_col = jnp.broadcast_to(l_scr[...], (PAGE, PAGE)).T[:nl, 0:1]
        o_ref[...] = _gather_heads(acc_scr[...] / l_col, nh, nq)


def fox_attn_decode(page_table, q8, kn8, vn8, lfn8, pool_k, pool_v, pool_lf, nh, nq):
    nb, n_pages = page_table.shape
    w = nh * HEAD_DIM
    n_chunks = n_pages // PAGES_PER_STEP
    row_spec = pl.BlockSpec((None, 8, w), lambda b, s, pt: (b, 0, 0))
    lf_specs = []
    for r in range(PAGES_PER_STEP):
        def imap(b, s, pt, r=r):
            c = jnp.minimum(s, n_chunks - 1)
            return (pt[b, n_pages - 1 - (c * PAGES_PER_STEP + r)], 0, 0)
        lf_specs.append(pl.BlockSpec((None, PAGE, nh), imap))
    grid_spec = pltpu.PrefetchScalarGridSpec(
        num_scalar_prefetch=1,
        grid=(nb, 2 * n_chunks),
        in_specs=[row_spec, row_spec, row_spec,
                  pl.BlockSpec((None, 8, PAGE), lambda b, s, pt: (b, 0, 0))]
        + _page_specs(w, n_pages, n_chunks, 0) + _page_specs(w, n_pages, n_chunks, 1) + lf_specs,
        out_specs=row_spec,
        scratch_shapes=[
            pltpu.VMEM((PAGE, w), BF16),
            pltpu.VMEM((n_pages * PAGE, PAGE), F32),
            pltpu.VMEM((8, PAGE), F32),
            pltpu.VMEM((PAGE, PAGE), F32),
            pltpu.VMEM((1, PAGE), F32),
            pltpu.VMEM((1, PAGE), F32),
            pltpu.VMEM((1, PAGE), F32),
            pltpu.VMEM((1, PAGE), F32),
            pltpu.VMEM((nq * nh, w), F32),
        ],
    )
    return pl.pallas_call(
        functools.partial(_fox_decode_kernel, nh=nh, nq=nq, n_chunks=n_chunks),
        grid_spec=grid_spec,
        out_shape=jax.ShapeDtypeStruct((nb, 8, w), F32),
        compiler_params=_cparams(("arbitrary", "arbitrary"), 56),
        name="fox_attn_decode",
    )(page_table, q8, kn8, vn8, lfn8, *([pool_k] * PAGES_PER_STEP), *([pool_v] * PAGES_PER_STEP),
      *([pool_lf] * PAGES_PER_STEP))


def _ssm_prep_kernel(are_ref, aim_ref, ldt_ref, bre_ref, bim_ref,
                     pwr_ref, pwi_ref, bbr_ref, bbi_ref):
    ar = are_ref[...]
    ai = aim_ref[...]
    dt = jnp.exp(ldt_ref[...])
    mag = jnp.exp(ar * dt)
    abr = mag * jnp.cos(ai * dt)
    abi = mag * jnp.sin(ai * dt)
    den = ar * ar + ai * ai
    nr = abr - 1.0
    ni = abi
    cr = (nr * ar + ni * ai) / den
    ci = (ni * ar - nr * ai) / den
    br = bre_ref[...]
    bi = bim_ref[...]
    bbr_ref[...] = cr * br - ci * bi
    bbi_ref[...] = cr * bi + ci * br
    pr, pi = abr, abi
    pwr_ref[0] = pr
    pwi_ref[0] = pi
    for k in range(1, SCAN_ROWS):
        pr, pi = pr * abr - pi * abi, pr * abi + pi * abr
        pwr_ref[k] = pr
        pwi_ref[k] = pi


def ssm_prep(a_re, a_im, log_dt, b_re_t, b_im_t):
    g, p = a_re.shape
    c = b_re_t.shape[1]
    return pl.pallas_call(
        _ssm_prep_kernel,
        out_shape=(jax.ShapeDtypeStruct((SCAN_ROWS, g, 1, p), F32),
                   jax.ShapeDtypeStruct((SCAN_ROWS, g, 1, p), F32),
                   jax.ShapeDtypeStruct((g, c, p), F32), jax.ShapeDtypeStruct((g, c, p), F32)),
        name="ssm_prep",
    )(a_re.reshape(g, 1, p), a_im.reshape(g, 1, p), log_dt.reshape(g, 1, 1), b_re_t, b_im_t)


def _ssm_prompt_kernel(u_ref, wbr_ref, wbi_ref, wcr_ref, wci_ref, d_ref, pwr_ref, pwi_ref,
                       y_ref, hr_ref, hi_ref, xr_scr, xi_scr, sr_scr, si_scr, *, tt):
    i = pl.program_id(2)
    lanes = xr_scr.shape[1]

    @pl.when(i == 0)
    def _():
        sr_scr[...] = jnp.zeros_like(sr_scr)
        si_scr[...] = jnp.zeros_like(si_scr)

    u = u_ref[...]
    ub = u.astype(BF16)
    xr_scr[...] = _dot(ub, wbr_ref[...])
    xi_scr[...] = _dot(ub, wbi_ref[...])

    pw_r = pwr_ref[...]
    pw_i = pwi_ref[...]
    row = lax.broadcasted_iota(jnp.int32, (SCAN_ROWS, lanes), 0)

    def tile(r, carry):
        h_r, h_i = carry
        off = pl.multiple_of(r * SCAN_ROWS, SCAN_ROWS)
        x_r = xr_scr[pl.ds(off, SCAN_ROWS), :]
        x_i = xi_scr[pl.ds(off, SCAN_ROWS), :]
        for sh in (1, 2, 4):
            a_r = pw_r[sh - 1:sh, :]
            a_i = pw_i[sh - 1:sh, :]
            p_r = jnp.where(row >= sh, pltpu.roll(x_r, sh, axis=0), 0.0)
            p_i = jnp.where(row >= sh, pltpu.roll(x_i, sh, axis=0), 0.0)
            x_r, x_i = x_r + (a_r * p_r - a_i * p_i), x_i + (a_r * p_i + a_i * p_r)
        x_r, x_i = x_r + (pw_r * h_r - pw_i * h_i), x_i + (pw_r * h_i + pw_i * h_r)
        xr_scr[pl.ds(off, SCAN_ROWS), :] = x_r
        xi_scr[pl.ds(off, SCAN_ROWS), :] = x_i
        return x_r[SCAN_ROWS - 1:SCAN_ROWS, :], x_i[SCAN_ROWS - 1:SCAN_ROWS, :]

    h_r, h_i = lax.fori_loop(0, tt // SCAN_ROWS, tile, (sr_scr[...], si_scr[...]))
    sr_scr[...] = h_r
    si_scr[...] = h_i
    hr_ref[...] = h_r
    hi_ref[...] = h_i
    y_ref[...] = (_dot(xr_scr[...].astype(BF16), wcr_ref[...])
                  - _dot(xi_scr[...].astype(BF16), wci_ref[...]) + d_ref[...] * u)


def ssm_prompt(proj, u_col, nb, seq, wbr, wbi, wcr, wci, d, pw_r, pw_i, tt=512):
    nchunk, cw, lanes = wbr.shape
    nt = seq // tt
    ucb = u_col * (nchunk)
    return pl.pallas_call(
        functools.partial(_ssm_prompt_kernel, tt=tt),
        grid=(nb, nchunk, nt),
        in_specs=[
            pl.BlockSpec((tt, cw), lambda b, j, i: (b * nt + i, ucb + j)),
            pl.BlockSpec((None, cw, lanes), lambda b, j, i: (j, 0, 0)),
            pl.BlockSpec((None, cw, lanes), lambda b, j, i: (j, 0, 0)),
            pl.BlockSpec((None, lanes, cw), lambda b, j, i: (j, 0, 0)),
            pl.BlockSpec((None, lanes, cw), lambda b, j, i: (j, 0, 0)),
            pl.BlockSpec((1, cw), lambda b, j, i: (0, j)),
            pl.BlockSpec((SCAN_ROWS, lanes), lambda b, j, i: (0, j)),
            pl.BlockSpec((SCAN_ROWS, lanes), lambda b, j, i: (0, j)),
        ],
        out_specs=(
            pl.BlockSpec((tt, cw), lambda b, j, i: (b * nt + i, j)),
            pl.BlockSpec((None, 1, lanes), lambda b, j, i: (b, 0, j)),
            pl.BlockSpec((None, 1, lanes), lambda b, j, i: (b, 0, j)),
        ),
        out_shape=(jax.ShapeDtypeStruct((nb * seq, nchunk * cw), F32),
                   jax.ShapeDtypeStruct((nb, 1, nchunk * lanes), F32),
                   jax.ShapeDtypeStruct((nb, 1, nchunk * lanes), F32)),
        scratch_shapes=[pltpu.VMEM((tt, lanes), F32), pltpu.VMEM((tt, lanes), F32),
                        pltpu.VMEM((1, lanes), F32), pltpu.VMEM((1, lanes), F32)],
        compiler_params=_cparams(("parallel", "parallel", "arbitrary"), 32),
        name="ssm_prompt",
    )(proj, wbr, wbi, wcr, wci, d, pw_r, pw_i)


def _ssm_sample_kernel(u_ref, h0r_ref, h0i_ref, wbr_ref, wbi_ref, wcr_ref, wci_ref, d_ref,
                       pwr_ref, pwi_ref, y_ref, hr_ref, hi_ref, *, steps):
    a_r = pwr_ref[0:1, :]
    a_i = pwi_ref[0:1, :]
    h_r = h0r_ref[...]
    h_i = h0i_ref[...]
    for t in range(steps):
        u = u_ref[t]
        ub = u.astype(BF16)
        h_r, h_i = (a_r * h_r - a_i * h_i + _dot(ub, wbr_ref[...]),
                    a_r * h_i + a_i * h_r + _dot(ub, wbi_ref[...]))
        y_ref[t] = (_dot(h_r.astype(BF16), wcr_ref[...]) - _dot(h_i.astype(BF16), wci_ref[...])
                    + d_ref[...] * u)
    hr_ref[...] = h_r
    hi_ref[...] = h_i


def ssm_sample(u_t, h0_r, h0_i, wbr, wbi, wcr, wci, d, pw_r, pw_i):
    steps, nb, _ = u_t.shape
    nchunk, cw, lanes = wbr.shape
    return pl.pallas_call(
        functools.partial(_ssm_sample_kernel, steps=steps),
        grid=(nchunk,),
        in_specs=[
            pl.BlockSpec((steps, nb, cw), lambda j: (0, 0, j)),
            pl.BlockSpec((nb, lanes), lambda j: (0, j)),
            pl.BlockSpec((nb, lanes), lambda j: (0, j)),
            pl.BlockSpec((None, cw, lanes), lambda j: (j, 0, 0)),
            pl.BlockSpec((None, cw, lanes), lambda j: (j, 0, 0)),
            pl.BlockSpec((None, lanes, cw), lambda j: (j, 0, 0)),
            pl.BlockSpec((None, lanes, cw), lambda j: (j, 0, 0)),
            pl.BlockSpec((1, cw), lambda j: (0, j)),
            pl.BlockSpec((SCAN_ROWS, lanes), lambda j: (0, j)),
            pl.BlockSpec((SCAN_ROWS, lanes), lambda j: (0, j)),
        ],
        out_specs=(
            pl.BlockSpec((steps, nb, cw), lambda j: (0, 0, j)),
            pl.BlockSpec((nb, lanes), lambda j: (0, j)),
            pl.BlockSpec((nb, lanes), lambda j: (0, j)),
        ),
        out_shape=(jax.ShapeDtypeStruct((steps, nb, nchunk * cw), F32),
                   jax.ShapeDtypeStruct((nb, nchunk * lanes), F32),
                   jax.ShapeDtypeStruct((nb, nchunk * lanes), F32)),
        compiler_params=_cparams(("parallel",), 32),
        name="ssm_sample",
    )(u_t, h0_r, h0_i, wbr, wbi, wcr, wci, d, pw_r, pw_i)


def _even_out_kernel(oa_ref, ga_ref, yb_ref, gb_ref, x_ref, wg_ref, bg_ref, wo_ref, o_ref):
    half = oa_ref.shape[1]
    ya = oa_ref[...] * jax.nn.silu(ga_ref[...])
    yb = jax.nn.gelu(yb_ref[...])
    yb = yb * jax.nn.sigmoid(_dot(yb.astype(BF16), wg_ref[...]) + bg_ref[...])
    yb = yb * jax.nn.silu(gb_ref[...])
    o_ref[...] = (x_ref[...] + _dot(ya.astype(BF16), wo_ref[0:half, :])
                  + _dot(yb.astype(BF16), wo_ref[half:, :]))


def even_out(o_a, y_b, proj, x, w_glu, b_glu, w_out, tm=256):
    m, half = o_a.shape
    d = x.shape[1]
    return pl.pallas_call(
        _even_out_kernel,
        grid=(m // tm,),
        in_specs=[
            pl.BlockSpec((tm, half), lambda i: (i, 0)),
            pl.BlockSpec((tm, half), lambda i: (i, 3)),
            pl.BlockSpec((tm, half), lambda i: (i, 0)),
            pl.BlockSpec((tm, half), lambda i: (i, 5)),
            pl.BlockSpec((tm, d), lambda i: (i, 0)),
            pl.BlockSpec((half, half), lambda i: (0, 0)),
            pl.BlockSpec((1, half), lambda i: (0, 0)),
            pl.BlockSpec((2 * half, d), lambda i: (0, 0)),
        ],
        out_specs=pl.BlockSpec((tm, d), lambda i: (i, 0)),
        out_shape=jax.ShapeDtypeStruct((m, d), F32),
        compiler_params=_cparams(("parallel",), 48),
        name="even_out",
    )(o_a, proj, y_b, proj, x, w_glu, b_glu, w_out)


def _odd_out_kernel(o_ref_in, g_ref, x_ref, wo_ref, gf_ref, y_ref):
    y = o_ref_in[...] * jax.nn.silu(g_ref[...])
    r = x_ref[...] + _dot(y.astype(BF16), wo_ref[...])
    n = r * lax.rsqrt(jnp.mean(r * r, axis=-1, keepdims=True) + RMS_EPS)
    y_ref[...] = n * gf_ref[...]


def odd_out(o, proj, x, w_out, g_final, tm=256):
    m, d = x.shape
    w = o.shape[1]
    return pl.pallas_call(
        _odd_out_kernel,
        grid=(m // tm,),
        in_specs=[
            pl.BlockSpec((tm, w), lambda i: (i, 0)),
            pl.BlockSpec((tm, w), lambda i: (i, 3)),
            pl.BlockSpec((tm, d), lambda i: (i, 0)),
            pl.BlockSpec((w, d), lambda i: (0, 0)),
            pl.BlockSpec((1, d), lambda i: (0, 0)),
        ],
        out_specs=pl.BlockSpec((tm, d), lambda i: (i, 0)),
        out_shape=jax.ShapeDtypeStruct((m, d), F32),
        compiler_params=_cparams(("parallel",), 48),
        name="odd_out",
    )(o, proj, x, w_out, g_final.reshape(1, d))


def _pad_rows8(x):
    return jnp.pad(x, ((0, 0), (0, 8 - x.shape[1]), (0, 0)))


def _block_diag(x):
    nc, pc, a, b = x.shape
    eye = jnp.eye(pc, dtype=x.dtype)
    return jnp.einsum("jgab,gh->jgahb", x, eye).reshape(nc, pc * a, pc * b)


def kernel(x_prompt, x_sample, cache_sb_k, cache_sb_v, state_ssm_re, state_ssm_im, cache_fox_k, cache_fox_v, cache_fox_logf, page_table, norm_g, w_in_even, ssm_a_re, ssm_a_im, ssm_log_dt, ssm_b_re, ssm_b_im, ssm_c_re, ssm_c_im, ssm_d, w_glu, b_glu, w_out_even, w_in_odd, b_f, w_out_odd, norm_f_g):
    nbp, seq, d_model = x_prompt.shape
    nbs, dec = x_sample.shape[:2]
    mp = nbp * seq
    ms = nbs * dec
    w_sb = d_model // 2
    h_sb = w_sb // HEAD_DIM
    w_ssm = d_model - w_sb
    g_ssm = w_ssm // SSM_GROUP
    h_fox = d_model // HEAD_DIM
    n_pool = cache_sb_k.shape[1]
    groups_per_chunk = HEAD_DIM // SSM_GROUP
    n_chunks = g_ssm // groups_per_chunk

    x0 = jnp.concatenate([x_prompt.reshape(mp, d_model), x_sample.reshape(ms, d_model)], axis=0)

    proj0 = norm_proj(x0, norm_g[0], w_in_even[0].astype(BF16), tm=512, tn=1024)
    proj0_s = proj0[mp:]
    k0_s = proj0_s[:, w_sb:2 * w_sb]
    v0_s = proj0_s[:, 2 * w_sb:3 * w_sb]

    oa_p = sb_attn_prompt(proj0, nbp, seq, h_sb)
    q8 = _pad_rows8(proj0_s[:, :w_sb].reshape(nbs, dec, w_sb))
    kn8 = _pad_rows8(k0_s.reshape(nbs, dec, w_sb))
    vn8 = _pad_rows8(v0_s.reshape(nbs, dec, w_sb))
    oa_s = sb_attn_decode(page_table, q8, kn8, vn8,
                          cache_sb_k.reshape(n_pool, PAGE, w_sb), cache_sb_v.reshape(n_pool, PAGE, w_sb),
                          h_sb, dec)[:, :dec].reshape(ms, w_sb)

    b_re_t = jnp.swapaxes(ssm_b_re[0], 1, 2)
    b_im_t = jnp.swapaxes(ssm_b_im[0], 1, 2)
    pw_r, pw_i, bb_r, bb_i = ssm_prep(ssm_a_re[0], ssm_a_im[0], ssm_log_dt[0], b_re_t, b_im_t)
    n_state = g_ssm * P_SSM
    pw_r = pw_r.reshape(SCAN_ROWS, n_state)
    pw_i = pw_i.reshape(SCAN_ROWS, n_state)
    wbr = _block_diag(bb_r.reshape(n_chunks, groups_per_chunk, SSM_GROUP, P_SSM)).astype(BF16)
    wbi = _block_diag(bb_i.reshape(n_chunks, groups_per_chunk, SSM_GROUP, P_SSM)).astype(BF16)
    c_re_t = jnp.swapaxes(ssm_c_re[0], 1, 2)
    c_im_t = jnp.swapaxes(ssm_c_im[0], 1, 2)
    wcr = _block_diag(c_re_t.reshape(n_chunks, groups_per_chunk, P_SSM, SSM_GROUP)).astype(BF16)
    wci = _block_diag(c_im_t.reshape(n_chunks, groups_per_chunk, P_SSM, SSM_GROUP)).astype(BF16)
    d_row = ssm_d[0].reshape(1, w_ssm)

    yb_p, hr_p, hi_p = ssm_prompt(proj0, 4, nbp, seq, wbr, wbi, wcr, wci, d_row, pw_r, pw_i)
    u_s = jnp.swapaxes(proj0_s[:, 4 * w_sb:4 * w_sb + w_ssm].reshape(nbs, dec, w_ssm), 0, 1)
    yb_s, hr_s, hi_s = ssm_sample(u_s, state_ssm_re[0].reshape(nbs, n_state),
                                  state_ssm_im[0].reshape(nbs, n_state),
                                  wbr, wbi, wcr, wci, d_row, pw_r, pw_i)
    yb_s = jnp.swapaxes(yb_s, 0, 1).reshape(ms, w_ssm)

    o_a = jnp.concatenate([oa_p, oa_s], axis=0)
    y_b = jnp.concatenate([yb_p, yb_s], axis=0)
    x1 = even_out(o_a, y_b, proj0, x0, w_glu[0].astype(BF16), b_glu[0].reshape(1, w_ssm),
                  w_out_even[0].astype(BF16))

    w_main = w_in_odd[0][:, :4 * d_model].astype(BF16)
    w_f = jnp.pad(w_in_odd[0][:, 4 * d_model:], ((0, 0), (0, PAGE - h_fox))).astype(BF16)
    b_pad = jnp.pad(b_f[0], (0, PAGE - h_fox)).reshape(1, PAGE)
    proj1 = norm_proj(x1, norm_g[1], w_main, tm=512, tn=1024)
    logf = fox_logf(x1, norm_g[1], w_f, b_pad, tm=512)
    logf_p = logf[:mp].reshape(nbp, seq, PAGE)
    logf_s = logf[mp:, :h_fox].reshape(nbs, dec, h_fox)

    cum = seq_cumsum(logf_p)[:, :, :h_fox]
    cum_row = jnp.transpose(cum, (0, 2, 1))[:, :, None, :]
    cum_col = jnp.transpose(cum, (0, 2, 1))[:, :, :, None]
    o_p = fox_attn_prompt(proj1, cum_col, cum_row, nbp, seq, h_fox)

    proj1_s = proj1[mp:]
    k1_s = proj1_s[:, d_model:2 * d_model]
    v1_s = proj1_s[:, 2 * d_model:3 * d_model]
    q8 = _pad_rows8(proj1_s[:, :d_model].reshape(nbs, dec, d_model))
    kn8 = _pad_rows8(k1_s.reshape(nbs, dec, d_model))
    vn8 = _pad_rows8(v1_s.reshape(nbs, dec, d_model))
    lfn8 = _pad_rows8(jnp.pad(jnp.tile(logf_s, (1, 1, dec)), ((0, 0), (0, 0), (0, PAGE - dec * h_fox))))
    o_s = fox_attn_decode(page_table, q8, kn8, vn8, lfn8,
                          cache_fox_k.reshape(n_pool, PAGE, d_model),
                          cache_fox_v.reshape(n_pool, PAGE, d_model),
                          cache_fox_logf.reshape(n_pool, PAGE, h_fox),
                          h_fox, dec)[:, :dec].reshape(ms, d_model)

    y = odd_out(jnp.concatenate([o_p, o_s], axis=0), proj1, x1, w_out_odd[0].astype(BF16), norm_f_g)

    proj0_p = proj0[:mp]
    proj1_p = proj1[:mp]
    y_prompt = y[:mp].reshape(nbp, seq, d_model)
    y_sample = y[mp:].reshape(nbs, dec, d_model)
    sb_k_prompt = proj0_p[:, w_sb:2 * w_sb].reshape(1, nbp, seq, h_sb, HEAD_DIM)
    sb_v_prompt = proj0_p[:, 2 * w_sb:3 * w_sb].reshape(1, nbp, seq, h_sb, HEAD_DIM)
    sb_k_sample = k0_s.reshape(1, nbs, dec, h_sb, HEAD_DIM)
    sb_v_sample = v0_s.reshape(1, nbs, dec, h_sb, HEAD_DIM)
    ssm_re_prompt = hr_p.reshape(1, nbp, g_ssm, P_SSM)
    ssm_im_prompt = hi_p.reshape(1, nbp, g_ssm, P_SSM)
    ssm_re_sample = hr_s.reshape(1, nbs, g_ssm, P_SSM)
    ssm_im_sample = hi_s.reshape(1, nbs, g_ssm, P_SSM)
    fox_k_prompt = proj1_p[:, d_model:2 * d_model].reshape(1, nbp, seq, h_fox, HEAD_DIM)
    fox_v_prompt = proj1_p[:, 2 * d_model:3 * d_model].reshape(1, nbp, seq, h_fox, HEAD_DIM)
    fox_logf_prompt = logf_p[:, :, :h_fox].reshape(1, nbp, seq, h_fox)
    fox_k_sample = k1_s.reshape(1, nbs, dec, h_fox, HEAD_DIM)
    fox_v_sample = v1_s.reshape(1, nbs, dec, h_fox, HEAD_DIM)
    fox_logf_sample = logf_s.reshape(1, nbs, dec, h_fox)
    return (y_prompt, y_sample, sb_k_prompt, sb_v_prompt, sb_k_sample, sb_v_sample,
            ssm_re_prompt, ssm_im_prompt, ssm_re_sample, ssm_im_sample,
            fox_k_prompt, fox_v_prompt, fox_logf_prompt, fox_k_sample, fox_v_sample, fox_logf_sample)
```

```python
import functools

import jax
import jax.numpy as jnp
from jax import lax
from jax.experimental import pallas as pl
from jax.experimental.pallas import tpu as pltpu

F32 = jnp.float32
BF16 = jnp.bfloat16

HEAD_DIM = 128
SSM_GROUP = 16
P_SSM = 64
RMS_EPS = 1e-6
PAGE = 128
HEADS_PER_BLOCK = 8
SB_PAGES_PER_STEP = 8
FOX_PAGES_PER_STEP = 4
SCAN_ROWS = 8
NEG_BIG = -1e30
MIB = 1024 * 1024


def _cparams(sem, vmem_mib):
    return pltpu.CompilerParams(dimension_semantics=sem, vmem_limit_bytes=vmem_mib * MIB)


def _split2(x):
    hi = x.astype(BF16)
    lo = (x - hi.astype(F32)).astype(BF16)
    return hi, lo


def _split3(x):
    hi = x.astype(BF16)
    r = x - hi.astype(F32)
    mid = r.astype(BF16)
    lo = (r - mid.astype(F32)).astype(BF16)
    return hi, mid, lo


def _dot(a, b):
    return jnp.dot(a, b, preferred_element_type=F32)


def _dot_nt(a, b):
    return lax.dot_general(a, b, (((1,), (1,)), ((), ())), preferred_element_type=F32)


def _neg_softplus(z):
    return -(jnp.maximum(z, 0.0) + jnp.log1p(jnp.exp(-jnp.abs(z))))


def _tri(n):
    r = lax.broadcasted_iota(jnp.int32, (n, n), 0)
    c = lax.broadcasted_iota(jnp.int32, (n, n), 1)
    return (r > c).astype(BF16)


def _norm_proj_kernel(x_ref, g_ref, w_ref, o_ref, h_scr):
    @pl.when(pl.program_id(1) == 0)
    def _():
        x = x_ref[...]
        y = x * lax.rsqrt(jnp.mean(x * x, axis=-1, keepdims=True) + RMS_EPS)
        h_scr[...] = (y * g_ref[...]).astype(BF16)

    o_ref[...] = _dot(h_scr[...], w_ref[...])


def norm_proj(x, g, w, tm, tn):
    m, d = x.shape
    n = w.shape[1]
    return pl.pallas_call(
        _norm_proj_kernel,
        grid=(m // tm, n // tn),
        in_specs=[
            pl.BlockSpec((tm, d), lambda i, j: (i, 0)),
            pl.BlockSpec((1, d), lambda i, j: (0, 0)),
            pl.BlockSpec((d, tn), lambda i, j: (0, j)),
        ],
        out_specs=pl.BlockSpec((tm, tn), lambda i, j: (i, j)),
        out_shape=jax.ShapeDtypeStruct((m, n), F32),
        scratch_shapes=[pltpu.VMEM((tm, d), BF16)],
        compiler_params=_cparams(("parallel", "arbitrary"), 48),
        name="norm_proj",
    )(x, g.reshape(1, d), w)


def _logf_kernel(x_ref, g_ref, w_ref, b_ref, o_ref):
    x = x_ref[...]
    y = x * lax.rsqrt(jnp.mean(x * x, axis=-1, keepdims=True) + RMS_EPS)
    h = (y * g_ref[...]).astype(BF16)
    f = _dot(h, w_ref[...]) + b_ref[...]
    o_ref[...] = _neg_softplus(-f)


def fox_logf(x, g, w_pad, b_pad, tm):
    m, d = x.shape
    n = w_pad.shape[1]
    return pl.pallas_call(
        _logf_kernel,
        grid=(m // tm,),
        in_specs=[
            pl.BlockSpec((tm, d), lambda i: (i, 0)),
            pl.BlockSpec((1, d), lambda i: (0, 0)),
            pl.BlockSpec((d, n), lambda i: (0, 0)),
            pl.BlockSpec((1, n), lambda i: (0, 0)),
        ],
        out_specs=pl.BlockSpec((tm, n), lambda i: (i, 0)),
        out_shape=jax.ShapeDtypeStruct((m, n), F32),
        compiler_params=_cparams(("parallel",), 32),
        name="fox_logf",
    )(x, g.reshape(1, d), w_pad, b_pad)


def _cumsum_kernel(x_ref, o_ref, *, blk):
    n = x_ref.shape[0]
    r = lax.broadcasted_iota(jnp.int32, (blk, blk), 0)
    c = lax.broadcasted_iota(jnp.int32, (blk, blk), 1)
    low = (c <= r).astype(BF16)

    def body(i, car):
        off = pl.multiple_of(i * blk, blk)
        hi, mid, lo = _split3(x_ref[pl.ds(off, blk), :])
        cum = _dot(low, hi) + _dot(low, mid) + _dot(low, lo) + car
        o_ref[pl.ds(off, blk), :] = cum
        return cum[blk - 1:blk, :]

    lax.fori_loop(0, n // blk, body, jnp.zeros((1, x_ref.shape[1]), F32))


def seq_cumsum(x):
    b, n, w = x.shape
    return pl.pallas_call(
        functools.partial(_cumsum_kernel, blk=128),
        grid=(b,),
        in_specs=[pl.BlockSpec((None, n, w), lambda i: (i, 0, 0))],
        out_specs=pl.BlockSpec((None, n, w), lambda i: (i, 0, 0)),
        out_shape=jax.ShapeDtypeStruct((b, n, w), F32),
        compiler_params=_cparams(("parallel",), 32),
        name="seq_cumsum",
    )(x)


def _sb_prompt_kernel(q_ref, k_ref, v_ref, o_ref, qb_scr, car_scr, *, tq, tk, hb):
    i = pl.program_id(2)
    qb_scr[...] = (q_ref[...] * (HEAD_DIM ** -0.5)).astype(BF16)
    o_ref[...] = jnp.zeros_like(o_ref)
    car_scr[...] = jnp.zeros_like(car_scr)
    t_pos = i * tq + lax.broadcasted_iota(jnp.int32, (tq, tk), 0)
    s_loc = lax.broadcasted_iota(jnp.int32, (tq, tk), 1)
    later = jnp.concatenate([_tri(tk), jnp.ones((tk, tk), BF16)], axis=1)
    nblk = (i + 1) * (tq // tk)

    def body(n, carry):
        off = pl.multiple_of((nblk - 1 - n) * tk, tk)
        before = (off + s_loc) < t_pos
        heads = [slice(h * HEAD_DIM, (h + 1) * HEAD_DIM) for h in range(hb)]
        zs = [_dot_nt(qb_scr[:, hc], k_ref[pl.ds(off, tk), hc].astype(BF16)) for hc in heads]
        lfs = [jnp.where(before, _neg_softplus(z), 0.0) for z in zs]
        sums = []
        for lf in lfs:
            hi, lo = _split2(lf)
            sums.append(_dot(hi, later) + _dot(lo, later))
        for h, hc in enumerate(heads):
            w = jnp.where(before, jnp.exp(zs[h] + lfs[h] + (sums[h][:, :tk] + car_scr[h])), 0.0)
            o_ref[:, hc] += _dot(w.astype(BF16), v_ref[pl.ds(off, tk), hc].astype(BF16))
            car_scr[h] += sums[h][:, tk:]
        return carry

    lax.fori_loop(0, nblk, body, 0)


def sb_attn_prompt(proj, nb, seq, nh, tq=256, tk=128, hb=4):
    nq = seq // tq
    ng = nh // hb
    wb = hb * HEAD_DIM
    return pl.pallas_call(
        functools.partial(_sb_prompt_kernel, tq=tq, tk=tk, hb=hb),
        grid=(nb, ng, nq),
        in_specs=[
            pl.BlockSpec((tq, wb), lambda b, g, i: (b * nq + i, g)),
            pl.BlockSpec((seq, wb), lambda b, g, i: (b, ng + g)),
            pl.BlockSpec((seq, wb), lambda b, g, i: (b, 2 * ng + g)),
        ],
        out_specs=pl.BlockSpec((tq, wb), lambda b, g, i: (b * nq + i, g)),
        out_shape=jax.ShapeDtypeStruct((nb * seq, nh * HEAD_DIM), F32),
        scratch_shapes=[pltpu.VMEM((tq, wb), BF16), pltpu.VMEM((hb, tq, tk), F32)],
        compiler_params=_cparams(("parallel", "parallel", "arbitrary"), 40),
        name="sb_attn_prompt",
    )(proj, proj, proj)


def _fox_prompt_kernel(q_ref, k_ref, v_ref, ccol_ref, crow_ref, o_ref, qb_scr, ct_scr, m_scr, l_scr,
                       *, tq, tk, hb):
    i = pl.program_id(2)
    reps = tk // HEAD_DIM
    qb_scr[...] = (q_ref[...] * (HEAD_DIM ** -0.5)).astype(BF16)
    o_ref[...] = jnp.zeros_like(o_ref)
    m_scr[...] = jnp.full(m_scr.shape, NEG_BIG, F32)
    l_scr[...] = jnp.zeros_like(l_scr)
    for h in range(hb):
        ct_scr[h] = jnp.broadcast_to(ccol_ref[h], (tq, HEAD_DIM))
    t_pos = i * tq + lax.broadcasted_iota(jnp.int32, (tq, tk), 0)
    s_loc = lax.broadcasted_iota(jnp.int32, (tq, tk), 1)
    ones = jnp.ones((tk, HEAD_DIM), BF16)
    nblk = (i + 1) * (tq // tk)

    def wide(x):
        return jnp.concatenate([x] * reps, axis=1)

    def body(j, carry):
        off = pl.multiple_of(j * tk, tk)
        causal = (off + s_loc) <= t_pos
        heads = [slice(h * HEAD_DIM, (h + 1) * HEAD_DIM) for h in range(hb)]
        zs = [_dot_nt(qb_scr[:, hc], k_ref[pl.ds(off, tk), hc].astype(BF16)) for hc in heads]
        ps = []
        alphas = []
        for h in range(hb):
            cs = crow_ref[h, :, pl.ds(off, tk)]
            s = jnp.where(causal, zs[h] + (wide(ct_scr[h]) - cs), NEG_BIG)
            m = m_scr[h]
            m_new = jnp.maximum(m, jnp.max(s, axis=1, keepdims=True))
            alphas.append(jnp.exp(m - m_new))
            ps.append(jnp.exp(s - wide(m_new)).astype(BF16))
            m_scr[h] = m_new
        for h, hc in enumerate(heads):
            l_scr[h] = alphas[h] * l_scr[h] + _dot(ps[h], ones)
            pv = _dot(ps[h], v_ref[pl.ds(off, tk), hc].astype(BF16))
            o_ref[:, hc] = alphas[h] * o_ref[:, hc] + pv
        return carry

    lax.fori_loop(0, nblk, body, 0)
    for h in range(hb):
        cols = slice(h * HEAD_DIM, (h + 1) * HEAD_DIM)
        o_ref[:, cols] = o_ref[:, cols] / l_scr[h]


def fox_attn_prompt(proj, cum_col, cum_row, nb, seq, nh, tq=256, tk=256, hb=4):
    nq = seq // tq
    ng = nh // hb
    wb = hb * HEAD_DIM
    return pl.pallas_call(
        functools.partial(_fox_prompt_kernel, tq=tq, tk=tk, hb=hb),
        grid=(nb, ng, nq),
        in_specs=[
            pl.BlockSpec((tq, wb), lambda b, g, i: (b * nq + i, g)),
            pl.BlockSpec((seq, wb), lambda b, g, i: (b, ng + g)),
            pl.BlockSpec((seq, wb), lambda b, g, i: (b, 2 * ng + g)),
            pl.BlockSpec((None, hb, tq, 1), lambda b, g, i: (b, g, i, 0)),
            pl.BlockSpec((None, hb, 1, seq), lambda b, g, i: (b, g, 0, 0)),
        ],
        out_specs=pl.BlockSpec((tq, wb), lambda b, g, i: (b * nq + i, g)),
        out_shape=jax.ShapeDtypeStruct((nb * seq, nh * HEAD_DIM), F32),
        scratch_shapes=[pltpu.VMEM((tq, wb), BF16), pltpu.VMEM((hb, tq, HEAD_DIM), F32),
                        pltpu.VMEM((hb, tq, HEAD_DIM), F32), pltpu.VMEM((hb, tq, HEAD_DIM), F32)],
        compiler_params=_cparams(("parallel", "parallel", "arbitrary"), 40),
        name="fox_attn_prompt",
    )(proj, proj, proj, cum_col, cum_row)


GROUP_LANES = HEADS_PER_BLOCK * HEAD_DIM


def _head_select():
    lane_head = lax.broadcasted_iota(jnp.int32, (HEADS_PER_BLOCK, GROUP_LANES), 1) // HEAD_DIM
    return lane_head == lax.broadcasted_iota(jnp.int32, (HEADS_PER_BLOCK, GROUP_LANES), 0)


def _block_diag_queries(q8, nh, nq):
    sel = _head_select()
    rows = []
    for t in range(nq):
        for g in range(nh // HEADS_PER_BLOCK):
            qg = q8[t:t + 1, g * GROUP_LANES:(g + 1) * GROUP_LANES]
            rows.append(jnp.where(sel, jnp.broadcast_to(qg, (HEADS_PER_BLOCK, GROUP_LANES)), 0.0))
    rows.append(jnp.zeros((PAGE - nq * nh, GROUP_LANES), F32))
    return jnp.concatenate(rows, axis=0)


def _gather_heads(acc, nh, nq):
    sel = _head_select()
    rows = []
    for t in range(nq):
        groups = []
        for g in range(nh // HEADS_PER_BLOCK):
            r0 = t * nh + g * HEADS_PER_BLOCK
            groups.append(jnp.sum(jnp.where(sel, acc[r0:r0 + HEADS_PER_BLOCK, :], 0.0),
                                  axis=0, keepdims=True))
        rows.append(jnp.concatenate(groups, axis=1))
    rows.append(jnp.zeros((8 - nq, nh * HEAD_DIM), F32))
    return jnp.concatenate(rows, axis=0)


def _new_rows_page(rows8, nh):
    parts = [rows8[:, g * GROUP_LANES:(g + 1) * GROUP_LANES] for g in range(nh // HEADS_PER_BLOCK)]
    parts.append(jnp.zeros((PAGE - 8 * len(parts), GROUP_LANES), F32))
    return jnp.concatenate(parts, axis=0).astype(BF16)


def _load_page(ref, nh):
    n_rows = PAGE * nh // HEADS_PER_BLOCK
    parts = [ref[pl.ds(j, n_rows, stride=HEADS_PER_BLOCK), :] for j in range(HEADS_PER_BLOCK)]
    return jnp.concatenate(parts, axis=1).astype(BF16)


def _suffix_sum(x):
    n = x.shape[0]
    row = lax.broadcasted_iota(jnp.int32, x.shape, 0)
    d = 1
    while d < n:
        x = x + jnp.where(row < n - d, pltpu.roll(x, n - d, axis=0), 0.0)
        d *= 2
    return x


def _sb_decode_kernel(pt_ref, q_ref, kn_ref, vn_ref, *refs, nh, nq, pps):
    k_refs = refs[:pps]
    v_refs = refs[pps:2 * pps]
    o_ref = refs[2 * pps]
    qbd_scr, car_scr, acc_scr = refs[2 * pps + 1:]
    s = pl.program_id(1)
    nl = nq * nh

    @pl.when(s == 0)
    def _():
        qbd = _block_diag_queries(q_ref[...] * (HEAD_DIM ** -0.5), nh, nq).T
        qbd_scr[...] = qbd.astype(BF16)
        key = lax.broadcasted_iota(jnp.int32, (PAGE, PAGE), 0)
        vis = key < lax.broadcasted_iota(jnp.int32, (PAGE, PAGE), 1) // nh
        z = _dot(_new_rows_page(kn_ref[...], nh), qbd_scr[...])
        suf = _suffix_sum(jnp.where(vis, _neg_softplus(z), 0.0))
        w = jnp.where(vis, jnp.exp(z + suf), 0.0)
        car_scr[...] = suf[0:1, :]
        acc_scr[...] = _dot(w.T[:nl, :].astype(BF16), _new_rows_page(vn_ref[...], nh))

    def scores(r):
        return _dot(_load_page(k_refs[r], nh), qbd_scr[...])

    ahead = 2
    zs = [scores(r) for r in range(min(ahead, pps))]
    car = car_scr[...]
    acc = acc_scr[...]
    for r in range(pps):
        suf = _suffix_sum(_neg_softplus(zs[r]))
        w = jnp.exp(zs[r] + suf + car)
        car = car + suf[0:1, :]
        if r + ahead < pps:
            zs.append(scores(r + ahead))
        acc = acc + _dot(w.T[:nl, :].astype(BF16), _load_page(v_refs[r], nh))
    car_scr[...] = car
    acc_scr[...] = acc

    @pl.when(s == pl.num_programs(1) - 1)
    def _():
        o_ref[...] = _gather_heads(acc_scr[...], nh, nq)


def _page_specs(rows, n_pages, pps):
    specs = []
    for r in range(pps):
        def imap(b, s, pt, r=r):
            return (pt[b, n_pages - 1 - (s * pps + r)], 0, 0)
        specs.append(pl.BlockSpec((None, rows, HEAD_DIM), imap))
    return specs


def sb_attn_decode(page_table, q8, kn8, vn8, pool_k, pool_v, nh, nq, pps):
    assert nh == HEADS_PER_BLOCK
    nb, n_pages = page_table.shape
    w = nh * HEAD_DIM
    row_spec = pl.BlockSpec((None, 8, w), lambda b, s, pt: (b, 0, 0))
    grid_spec = pltpu.PrefetchScalarGridSpec(
        num_scalar_prefetch=1,
        grid=(nb, n_pages // pps),
        in_specs=[row_spec, row_spec, row_spec]
        + _page_specs(PAGE * nh, n_pages, pps) + _page_specs(PAGE * nh, n_pages, pps),
        out_specs=row_spec,
        scratch_shapes=[
            pltpu.VMEM((GROUP_LANES, PAGE), BF16),
            pltpu.VMEM((1, PAGE), F32),
            pltpu.VMEM((nq * nh, GROUP_LANES), F32),
        ],
    )
    return pl.pallas_call(
        functools.partial(_sb_decode_kernel, nh=nh, nq=nq, pps=pps),
        grid_spec=grid_spec,
        out_shape=jax.ShapeDtypeStruct((nb, 8, w), F32),
        compiler_params=_cparams(("arbitrary", "arbitrary"), 56),
        name="sb_attn_decode",
    )(page_table, q8, kn8, vn8, *([pool_k] * pps), *([pool_v] * pps))


def _tile_lanes(x16, nh, nq):
    out = x16
    for t in range(1, nq):
        out = out + pltpu.roll(x16, t * nh, axis=1)
    return out


def _fox_decode_kernel(pt_ref, q_ref, kn_ref, vn_ref, lfn_ref, *refs, nh, nq, pps):
    ng = nh // HEADS_PER_BLOCK
    k_refs = refs[:pps]
    v_refs = refs[pps:2 * pps]
    lf_refs = refs[2 * pps:3 * pps]
    o_ref = refs[3 * pps]
    qbd_scr, lfpad_scr, bias_scr, car_scr, pre_scr, m_scr, l_scr, acc_scr = refs[3 * pps + 1:]
    s = pl.program_id(1)
    nl = nq * nh
    n_rows = PAGE * ng
    row_group = lax.broadcasted_iota(jnp.int32, (n_rows, PAGE), 0) % ng
    lane_group = (lax.broadcasted_iota(jnp.int32, (n_rows, PAGE), 1) % nh) // HEADS_PER_BLOCK
    valid = row_group == lane_group

    def to_col(row):
        return jnp.broadcast_to(row, (PAGE, PAGE)).T[:nl, 0:1]

    def to_rows(p):
        blocks = [p[i * PAGE:(i + 1) * PAGE, :].T for i in range(p.shape[0] // PAGE)]
        return jnp.concatenate(blocks, axis=1)[:nl, :].astype(BF16)

    @pl.when(s == 0)
    def _():
        qbd = _block_diag_queries(q_ref[...] * (HEAD_DIM ** -0.5), nh, nq).T
        qbd_scr[...] = qbd.astype(BF16)
        lfpad_scr[...] = jnp.zeros((PAGE, PAGE), F32)
        lfn = lfn_ref[...]
        key8 = lax.broadcasted_iota(jnp.int32, (8, PAGE), 0)
        qry8 = lax.broadcasted_iota(jnp.int32, (8, PAGE), 1) // nh
        pre_q = jnp.sum(jnp.where(key8 <= qry8, lfn, 0.0), axis=0, keepdims=True)
        run = jnp.zeros((1, PAGE), F32)
        pre_rows = []
        for j in range(8):
            run = run + lfn[j:j + 1, :]
            pre_rows.append(run)
        pre_k = jnp.concatenate(pre_rows, axis=0)
        nr = 8 * ng
        rown = lax.broadcasted_iota(jnp.int32, (nr, PAGE), 0)
        lanen = lax.broadcasted_iota(jnp.int32, (nr, PAGE), 1)
        seen = ((rown // 8) == (lanen % nh) // HEADS_PER_BLOCK) & ((rown % 8) <= lanen // nh)
        z = _dot(_new_rows_page(kn_ref[...], nh), qbd_scr[...])[:nr, :]
        sc = jnp.where(seen, z + jnp.concatenate([pre_q - pre_k] * ng, axis=0), NEG_BIG)
        m = jnp.max(sc, axis=0, keepdims=True)
        p = jnp.exp(sc - m)
        pt = to_rows(jnp.concatenate([p, jnp.zeros((PAGE - nr, PAGE), F32)], axis=0))
        acc_scr[...] = _dot(pt, _new_rows_page(vn_ref[...], nh))
        l_scr[...] = jnp.sum(p, axis=0, keepdims=True)
        m_scr[...] = m
        pre_scr[...] = pre_q
        car_scr[...] = jnp.zeros((1, PAGE), F32)

    zs = [_dot(_load_page(k_refs[r], nh), qbd_scr[...]) for r in range(pps)]
    car = car_scr[...]
    pre_q = pre_scr[...]
    scores = []
    for r in range(pps):
        lfpad_scr[:, 0:nh] = lf_refs[r][...]
        lf = _tile_lanes(lfpad_scr[...], nh, nq)
        suf = _suffix_sum(lf)
        bias = (suf - lf) + car + pre_q
        for g in range(ng):
            bias_scr[pl.ds(g, PAGE, stride=ng), :] = bias
        scores.append(jnp.where(valid, zs[r] + bias_scr[...], NEG_BIG))
        car = car + suf[0:1, :]
    car_scr[...] = car

    m_old = m_scr[...]
    m_new = m_old
    for sc in scores:
        m_new = jnp.maximum(m_new, jnp.max(sc, axis=0, keepdims=True))
    alpha = jnp.exp(m_old - m_new)
    l = alpha * l_scr[...]
    acc = acc_scr[...] * to_col(alpha)
    for r in range(pps):
        p = jnp.exp(scores[r] - m_new)
        l = l + jnp.sum(p, axis=0, keepdims=True)
        acc = acc + _dot(to_rows(p), _load_page(v_refs[r], nh))
    acc_scr[...] = acc
    l_scr[...] = l
    m_scr[...] = m_new

    @pl.when(s == pl.num_programs(1) - 1)
    def _():
        o_ref[...] = _gather_heads(acc_scr[...] / to_col(l_scr[...]), nh, nq)


def fox_attn_decode(page_table, q8, kn8, vn8, lfn8, pool_k, pool_v, pool_lf, nh, nq, pps):
    nb, n_pages = page_table.shape
    w = nh * HEAD_DIM
    ng = nh // HEADS_PER_BLOCK
    row_spec = pl.BlockSpec((None, 8, w), lambda b, s, pt: (b, 0, 0))
    lf_specs = []
    for r in range(pps):
        def imap(b, s, pt, r=r):
            return (pt[b, n_pages - 1 - (s * pps + r)], 0, 0)
        lf_specs.append(pl.BlockSpec((None, PAGE, nh), imap))
    grid_spec = pltpu.PrefetchScalarGridSpec(
        num_scalar_prefetch=1,
        grid=(nb, n_pages // pps),
        in_specs=[row_spec, row_spec, row_spec,
                  pl.BlockSpec((None, 8, PAGE), lambda b, s, pt: (b, 0, 0))]
        + _page_specs(PAGE * nh, n_pages, pps) + _page_specs(PAGE * nh, n_pages, pps) + lf_specs,
        out_specs=row_spec,
        scratch_shapes=[
            pltpu.VMEM((GROUP_LANES, PAGE), BF16),
            pltpu.VMEM((PAGE, PAGE), F32),
            pltpu.VMEM((PAGE * ng, PAGE), F32),
            pltpu.VMEM((1, PAGE), F32),
            pltpu.VMEM((1, PAGE), F32),
            pltpu.VMEM((1, PAGE), F32),
            pltpu.VMEM((1, PAGE), F32),
            pltpu.VMEM((nq * nh, GROUP_LANES), F32),
        ],
    )
    return pl.pallas_call(
        functools.partial(_fox_decode_kernel, nh=nh, nq=nq, pps=pps),
        grid_spec=grid_spec,
        out_shape=jax.ShapeDtypeStruct((nb, 8, w), F32),
        compiler_params=_cparams(("arbitrary", "arbitrary"), 56),
        name="fox_attn_decode",
    )(page_table, q8, kn8, vn8, lfn8, *([pool_k] * pps), *([pool_v] * pps), *([pool_lf] * pps))


def _ssm_prep_kernel(are_ref, aim_ref, ldt_ref, bre_ref, bim_ref,
                     pwr_ref, pwi_ref, bbr_ref, bbi_ref):
    ar = are_ref[...]
    ai = aim_ref[...]
    dt = jnp.exp(ldt_ref[...])
    mag = jnp.exp(ar * dt)
    abr = mag * jnp.cos(ai * dt)
    abi = mag * jnp.sin(ai * dt)
    den = ar * ar + ai * ai
    nr = abr - 1.0
    ni = abi
    cr = (nr * ar + ni * ai) / den
    ci = (ni * ar - nr * ai) / den
    br = bre_ref[...]
    bi = bim_ref[...]
    bbr_ref[...] = cr * br - ci * bi
    bbi_ref[...] = cr * bi + ci * br
    pr, pi = abr, abi
    pwr_ref[0] = pr
    pwi_ref[0] = pi
    for k in range(1, SCAN_ROWS):
        pr, pi = pr * abr - pi * abi, pr * abi + pi * abr
        pwr_ref[k] = pr
        pwi_ref[k] = pi


def ssm_prep(a_re, a_im, log_dt, b_re_t, b_im_t):
    g, p = a_re.shape
    c = b_re_t.shape[1]
    return pl.pallas_call(
        _ssm_prep_kernel,
        out_shape=(jax.ShapeDtypeStruct((SCAN_ROWS, g, 1, p), F32),
                   jax.ShapeDtypeStruct((SCAN_ROWS, g, 1, p), F32),
                   jax.ShapeDtypeStruct((g, c, p), F32), jax.ShapeDtypeStruct((g, c, p), F32)),
        name="ssm_prep",
    )(a_re.reshape(g, 1, p), a_im.reshape(g, 1, p), log_dt.reshape(g, 1, 1), b_re_t, b_im_t)


def _ssm_prompt_kernel(u_ref, wbr_ref, wbi_ref, wcr_ref, wci_ref, d_ref, pwr_ref, pwi_ref,
                       y_ref, hr_ref, hi_ref, xr_scr, xi_scr, sr_scr, si_scr, *, tt):
    i = pl.program_id(2)
    lanes = xr_scr.shape[1]

    @pl.when(i == 0)
    def _():
        sr_scr[...] = jnp.zeros_like(sr_scr)
        si_scr[...] = jnp.zeros_like(si_scr)

    u = u_ref[...]
    ub = u.astype(BF16)
    xr_scr[...] = _dot(ub, wbr_ref[...])
    xi_scr[...] = _dot(ub, wbi_ref[...])

    pw_r = pwr_ref[...]
    pw_i = pwi_ref[...]
    row = lax.broadcasted_iota(jnp.int32, (SCAN_ROWS, lanes), 0)

    def tile(r, carry):
        h_r, h_i = carry
        off = pl.multiple_of(r * SCAN_ROWS, SCAN_ROWS)
        x_r = xr_scr[pl.ds(off, SCAN_ROWS), :]
        x_i = xi_scr[pl.ds(off, SCAN_ROWS), :]
        for sh in (1, 2, 4):
            a_r = pw_r[sh - 1:sh, :]
            a_i = pw_i[sh - 1:sh, :]
            p_r = jnp.where(row >= sh, pltpu.roll(x_r, sh, axis=0), 0.0)
            p_i = jnp.where(row >= sh, pltpu.roll(x_i, sh, axis=0), 0.0)
            x_r, x_i = x_r + (a_r * p_r - a_i * p_i), x_i + (a_r * p_i + a_i * p_r)
        x_r, x_i = x_r + (pw_r * h_r - pw_i * h_i), x_i + (pw_r * h_i + pw_i * h_r)
        xr_scr[pl.ds(off, SCAN_ROWS), :] = x_r
        xi_scr[pl.ds(off, SCAN_ROWS), :] = x_i
        return x_r[SCAN_ROWS - 1:SCAN_ROWS, :], x_i[SCAN_ROWS - 1:SCAN_ROWS, :]

    h_r, h_i = lax.fori_loop(0, tt // SCAN_ROWS, tile, (sr_scr[...], si_scr[...]))
    sr_scr[...] = h_r
    si_scr[...] = h_i
    hr_ref[...] = h_r
    hi_ref[...] = h_i
    y_ref[...] = (_dot(xr_scr[...].astype(BF16), wcr_ref[...])
                  - _dot(xi_scr[...].astype(BF16), wci_ref[...]) + d_ref[...] * u)


def ssm_prompt(proj, u_col, nb, seq, wbr, wbi, wcr, wci, d, pw_r, pw_i, tt=512):
    nchunk, cw, lanes = wbr.shape
    nt = seq // tt
    ucb = u_col * (nchunk)
    return pl.pallas_call(
        functools.partial(_ssm_prompt_kernel, tt=tt),
        grid=(nb, nchunk, nt),
        in_specs=[
            pl.BlockSpec((tt, cw), lambda b, j, i: (b * nt + i, ucb + j)),
            pl.BlockSpec((None, cw, lanes), lambda b, j, i: (j, 0, 0)),
            pl.BlockSpec((None, cw, lanes), lambda b, j, i: (j, 0, 0)),
            pl.BlockSpec((None, lanes, cw), lambda b, j, i: (j, 0, 0)),
            pl.BlockSpec((None, lanes, cw), lambda b, j, i: (j, 0, 0)),
            pl.BlockSpec((1, cw), lambda b, j, i: (0, j)),
            pl.BlockSpec((SCAN_ROWS, lanes), lambda b, j, i: (0, j)),
            pl.BlockSpec((SCAN_ROWS, lanes), lambda b, j, i: (0, j)),
        ],
        out_specs=(
            pl.BlockSpec((tt, cw), lambda b, j, i: (b * nt + i, j)),
            pl.BlockSpec((None, 1, lanes), lambda b, j, i: (b, 0, j)),
            pl.BlockSpec((None, 1, lanes), lambda b, j, i: (b, 0, j)),
        ),
        out_shape=(jax.ShapeDtypeStruct((nb * seq, nchunk * cw), F32),
                   jax.ShapeDtypeStruct((nb, 1, nchunk * lanes), F32),
                   jax.ShapeDtypeStruct((nb, 1, nchunk * lanes), F32)),
        scratch_shapes=[pltpu.VMEM((tt, lanes), F32), pltpu.VMEM((tt, lanes), F32),
                        pltpu.VMEM((1, lanes), F32), pltpu.VMEM((1, lanes), F32)],
        compiler_params=_cparams(("parallel", "parallel", "arbitrary"), 32),
        name="ssm_prompt",
    )(proj, wbr, wbi, wcr, wci, d, pw_r, pw_i)


def _ssm_sample_kernel(u_ref, h0r_ref, h0i_ref, wbr_ref, wbi_ref, wcr_ref, wci_ref, d_ref,
                       pwr_ref, pwi_ref, y_ref, hr_ref, hi_ref, *, steps):
    a_r = pwr_ref[0:1, :]
    a_i = pwi_ref[0:1, :]
    h_r = h0r_ref[...]
    h_i = h0i_ref[...]
    for t in range(steps):
        u = u_ref[t]
        ub = u.astype(BF16)
        h_r, h_i = (a_r * h_r - a_i * h_i + _dot(ub, wbr_ref[...]),
                    a_r * h_i + a_i * h_r + _dot(ub, wbi_ref[...]))
        y_ref[t] = (_dot(h_r.astype(BF16), wcr_ref[...]) - _dot(h_i.astype(BF16), wci_ref[...])
                    + d_ref[...] * u)
    hr_ref[...] = h_r
    hi_ref[...] = h_i


def ssm_sample(u_t, h0_r, h0_i, wbr, wbi, wcr, wci, d, pw_r, pw_i):
    steps, nb, _ = u_t.shape
    nchunk, cw, lanes = wbr.shape
    return pl.pallas_call(
        functools.partial(_ssm_sample_kernel, steps=steps),
        grid=(nchunk,),
        in_specs=[
            pl.BlockSpec((steps, nb, cw), lambda j: (0, 0, j)),
            pl.BlockSpec((nb, lanes), lambda j: (0, j)),
            pl.BlockSpec((nb, lanes), lambda j: (0, j)),
            pl.BlockSpec((None, cw, lanes), lambda j: (j, 0, 0)),
            pl.BlockSpec((None, cw, lanes), lambda j: (j, 0, 0)),
            pl.BlockSpec((None, lanes, cw), lambda j: (j, 0, 0)),
            pl.BlockSpec((None, lanes, cw), lambda j: (j, 0, 0)),
            pl.BlockSpec((1, cw), lambda j: (0, j)),
            pl.BlockSpec((SCAN_ROWS, lanes), lambda j: (0, j)),
            pl.BlockSpec((SCAN_ROWS, lanes), lambda j: (0, j)),
        ],
        out_specs=(
            pl.BlockSpec((steps, nb, cw), lambda j: (0, 0, j)),
            pl.BlockSpec((nb, lanes), lambda j: (0, j)),
            pl.BlockSpec((nb, lanes), lambda j: (0, j)),
        ),
        out_shape=(jax.ShapeDtypeStruct((steps, nb, nchunk * cw), F32),
                   jax.ShapeDtypeStruct((nb, nchunk * lanes), F32),
                   jax.ShapeDtypeStruct((nb, nchunk * lanes), F32)),
        compiler_params=_cparams(("parallel",), 32),
        name="ssm_sample",
    )(u_t, h0_r, h0_i, wbr, wbi, wcr, wci, d, pw_r, pw_i)


def _even_out_kernel(oa_ref, ga_ref, yb_ref, gb_ref, x_ref, wg_ref, bg_ref, wo_ref, o_ref):
    half = oa_ref.shape[1]
    ya = oa_ref[...] * jax.nn.silu(ga_ref[...])
    yb = jax.nn.gelu(yb_ref[...])
    yb = yb * jax.nn.sigmoid(_dot(yb.astype(BF16), wg_ref[...]) + bg_ref[...])
    yb = yb * jax.nn.silu(gb_ref[...])
    o_ref[...] = (x_ref[...] + _dot(ya.astype(BF16), wo_ref[0:half, :])
                  + _dot(yb.astype(BF16), wo_ref[half:, :]))


def even_out(o_a, y_b, proj, x, w_glu, b_glu, w_out, tm=256):
    m, half = o_a.shape
    d = x.shape[1]
    return pl.pallas_call(
        _even_out_kernel,
        grid=(m // tm,),
        in_specs=[
            pl.BlockSpec((tm, half), lambda i: (i, 0)),
            pl.BlockSpec((tm, half), lambda i: (i, 3)),
            pl.BlockSpec((tm, half), lambda i: (i, 0)),
            pl.BlockSpec((tm, half), lambda i: (i, 5)),
            pl.BlockSpec((tm, d), lambda i: (i, 0)),
            pl.BlockSpec((half, half), lambda i: (0, 0)),
            pl.BlockSpec((1, half), lambda i: (0, 0)),
            pl.BlockSpec((2 * half, d), lambda i: (0, 0)),
        ],
        out_specs=pl.BlockSpec((tm, d), lambda i: (i, 0)),
        out_shape=jax.ShapeDtypeStruct((m, d), F32),
        compiler_params=_cparams(("parallel",), 48),
        name="even_out",
    )(o_a, proj, y_b, proj, x, w_glu, b_glu, w_out)


def _odd_out_kernel(o_ref_in, g_ref, x_ref, wo_ref, gf_ref, y_ref):
    y = o_ref_in[...] * jax.nn.silu(g_ref[...])
    r = x_ref[...] + _dot(y.astype(BF16), wo_ref[...])
    n = r * lax.rsqrt(jnp.mean(r * r, axis=-1, keepdims=True) + RMS_EPS)
    y_ref[...] = n * gf_ref[...]


def odd_out(o, proj, x, w_out, g_final, tm=256):
    m, d = x.shape
    w = o.shape[1]
    return pl.pallas_call(
        _odd_out_kernel,
        grid=(m // tm,),
        in_specs=[
            pl.BlockSpec((tm, w), lambda i: (i, 0)),
            pl.BlockSpec((tm, w), lambda i: (i, 3)),
            pl.BlockSpec((tm, d), lambda i: (i, 0)),
            pl.BlockSpec((w, d), lambda i: (0, 0)),
            pl.BlockSpec((1, d), lambda i: (0, 0)),
        ],
        out_specs=pl.BlockSpec((tm, d), lambda i: (i, 0)),
        out_shape=jax.ShapeDtypeStruct((m, d), F32),
        compiler_params=_cparams(("parallel",), 48),
        name="odd_out",
    )(o, proj, x, w_out, g_final.reshape(1, d))


def _pad_rows8(x):
    return jnp.pad(x, ((0, 0), (0, 8 - x.shape[1]), (0, 0)))


def _block_diag(x):
    nc, pc, a, b = x.shape
    eye = jnp.eye(pc, dtype=x.dtype)
    return jnp.einsum("jgab,gh->jgahb", x, eye).reshape(nc, pc * a, pc * b)


def kernel(x_prompt, x_sample, cache_sb_k, cache_sb_v, state_ssm_re, state_ssm_im, cache_fox_k, cache_fox_v, cache_fox_logf, page_table, norm_g, w_in_even, ssm_a_re, ssm_a_im, ssm_log_dt, ssm_b_re, ssm_b_im, ssm_c_re, ssm_c_im, ssm_d, w_glu, b_glu, w_out_even, w_in_odd, b_f, w_out_odd, norm_f_g):
    nbp, seq, d_model = x_prompt.shape
    nbs, dec = x_sample.shape[:2]
    mp = nbp * seq
    ms = nbs * dec
    w_sb = d_model // 2
    h_sb = w_sb // HEAD_DIM
    w_ssm = d_model - w_sb
    g_ssm = w_ssm // SSM_GROUP
    h_fox = d_model // HEAD_DIM
    n_pool = cache_sb_k.shape[1]
    groups_per_chunk = HEAD_DIM // SSM_GROUP
    n_chunks = g_ssm // groups_per_chunk

    x0 = jnp.concatenate([x_prompt.reshape(mp, d_model), x_sample.reshape(ms, d_model)], axis=0)

    proj0 = norm_proj(x0, norm_g[0], w_in_even[0].astype(BF16), tm=512, tn=1024)
    proj0_s = proj0[mp:]
    k0_s = proj0_s[:, w_sb:2 * w_sb]
    v0_s = proj0_s[:, 2 * w_sb:3 * w_sb]

    oa_p = sb_attn_prompt(proj0, nbp, seq, h_sb)
    q8 = _pad_rows8(proj0_s[:, :w_sb].reshape(nbs, dec, w_sb))
    kn8 = _pad_rows8(k0_s.reshape(nbs, dec, w_sb))
    vn8 = _pad_rows8(v0_s.reshape(nbs, dec, w_sb))
    oa_s = sb_attn_decode(page_table, q8, kn8, vn8,
                          cache_sb_k.reshape(n_pool, PAGE * h_sb, HEAD_DIM),
                          cache_sb_v.reshape(n_pool, PAGE * h_sb, HEAD_DIM),
                          h_sb, dec, SB_PAGES_PER_STEP)[:, :dec].reshape(ms, w_sb)

    b_re_t = jnp.swapaxes(ssm_b_re[0], 1, 2)
    b_im_t = jnp.swapaxes(ssm_b_im[0], 1, 2)
    pw_r, pw_i, bb_r, bb_i = ssm_prep(ssm_a_re[0], ssm_a_im[0], ssm_log_dt[0], b_re_t, b_im_t)
    n_state = g_ssm * P_SSM
    pw_r = pw_r.reshape(SCAN_ROWS, n_state)
    pw_i = pw_i.reshape(SCAN_ROWS, n_state)
    wbr = _block_diag(bb_r.reshape(n_chunks, groups_per_chunk, SSM_GROUP, P_SSM)).astype(BF16)
    wbi = _block_diag(bb_i.reshape(n_chunks, groups_per_chunk, SSM_GROUP, P_SSM)).astype(BF16)
    c_re_t = jnp.swapaxes(ssm_c_re[0], 1, 2)
    c_im_t = jnp.swapaxes(ssm_c_im[0], 1, 2)
    wcr = _block_diag(c_re_t.reshape(n_chunks, groups_per_chunk, P_SSM, SSM_GROUP)).astype(BF16)
    wci = _block_diag(c_im_t.reshape(n_chunks, groups_per_chunk, P_SSM, SSM_GROUP)).astype(BF16)
    d_row = ssm_d[0].reshape(1, w_ssm)

    yb_p, hr_p, hi_p = ssm_prompt(proj0, 4, nbp, seq, wbr, wbi, wcr, wci, d_row, pw_r, pw_i)
    u_s = jnp.swapaxes(proj0_s[:, 4 * w_sb:4 * w_sb + w_ssm].reshape(nbs, dec, w_ssm), 0, 1)
    yb_s, hr_s, hi_s = ssm_sample(u_s, state_ssm_re[0].reshape(nbs, n_state),
                                  state_ssm_im[0].reshape(nbs, n_state),
                                  wbr, wbi, wcr, wci, d_row, pw_r, pw_i)
    yb_s = jnp.swapaxes(yb_s, 0, 1).reshape(ms, w_ssm)

    o_a = jnp.concatenate([oa_p, oa_s], axis=0)
    y_b = jnp.concatenate([yb_p, yb_s], axis=0)
    x1 = even_out(o_a, y_b, proj0, x0, w_glu[0].astype(BF16), b_glu[0].reshape(1, w_ssm),
                  w_out_even[0].astype(BF16))

    w_main = w_in_odd[0][:, :4 * d_model].astype(BF16)
    w_f = jnp.pad(w_in_odd[0][:, 4 * d_model:], ((0, 0), (0, PAGE - h_fox))).astype(BF16)
    b_pad = jnp.pad(b_f[0], (0, PAGE - h_fox)).reshape(1, PAGE)
    proj1 = norm_proj(x1, norm_g[1], w_main, tm=512, tn=1024)
    logf = fox_logf(x1, norm_g[1], w_f, b_pad, tm=512)
    logf_p = logf[:mp].reshape(nbp, seq, PAGE)
    logf_s = logf[mp:, :h_fox].reshape(nbs, dec, h_fox)

    cum = seq_cumsum(logf_p)[:, :, :h_fox]
    cum_row = jnp.transpose(cum, (0, 2, 1))[:, :, None, :]
    cum_col = jnp.transpose(cum, (0, 2, 1))[:, :, :, None]
    o_p = fox_attn_prompt(proj1, cum_col, cum_row, nbp, seq, h_fox)

    proj1_s = proj1[mp:]
    k1_s = proj1_s[:, d_model:2 * d_model]
    v1_s = proj1_s[:, 2 * d_model:3 * d_model]
    q8 = _pad_rows8(proj1_s[:, :d_model].reshape(nbs, dec, d_model))
    kn8 = _pad_rows8(k1_s.reshape(nbs, dec, d_model))
    vn8 = _pad_rows8(v1_s.reshape(nbs, dec, d_model))
    lfn8 = _pad_rows8(jnp.pad(jnp.tile(logf_s, (1, 1, dec)), ((0, 0), (0, 0), (0, PAGE - dec * h_fox))))
    o_s = fox_attn_decode(page_table, q8, kn8, vn8, lfn8,
                          cache_fox_k.reshape(n_pool, PAGE * h_fox, HEAD_DIM),
                          cache_fox_v.reshape(n_pool, PAGE * h_fox, HEAD_DIM),
                          cache_fox_logf.reshape(n_pool, PAGE, h_fox),
                          h_fox, dec, FOX_PAGES_PER_STEP)[:, :dec].reshape(ms, d_model)

    y = odd_out(jnp.concatenate([o_p, o_s], axis=0), proj1, x1, w_out_odd[0].astype(BF16), norm_f_g)

    proj0_p = proj0[:mp]
    proj1_p = proj1[:mp]
    y_prompt = y[:mp].reshape(nbp, seq, d_model)
    y_sample = y[mp:].reshape(nbs, dec, d_model)
    sb_k_prompt = proj0_p[:, w_sb:2 * w_sb].reshape(1, nbp, seq, h_sb, HEAD_DIM)
    sb_v_prompt = proj0_p[:, 2 * w_sb:3 * w_sb].reshape(1, nbp, seq, h_sb, HEAD_DIM)
    sb_k_sample = k0_s.reshape(1, nbs, dec, h_sb, HEAD_DIM)
    sb_v_sample = v0_s.reshape(1, nbs, dec, h_sb, HEAD_DIM)
    ssm_re_prompt = hr_p.reshape(1, nbp, g_ssm, P_SSM)
    ssm_im_prompt = hi_p.reshape(1, nbp, g_ssm, P_SSM)
    ssm_re_sample = hr_s.reshape(1, nbs, g_ssm, P_SSM)
    ssm_im_sample = hi_s.reshape(1, nbs, g_ssm, P_SSM)
    fox_k_prompt = proj1_p[:, d_model:2 * d_model].reshape(1, nbp, seq, h_fox, HEAD_DIM)
    fox_v_prompt = proj1_p[:, 2 * d_model:3 * d_model].reshape(1, nbp, seq, h_fox, HEAD_DIM)
    fox_logf_prompt = logf_p[:, :, :h_fox].reshape(1, nbp, seq, h_fox)
    fox_k_sample = k1_s.reshape(1, nbs, dec, h_fox, HEAD_DIM)
    fox_v_sample = v1_s.reshape(1, nbs, dec, h_fox, HEAD_DIM)
    fox_logf_sample = logf_s.reshape(1, nbs, dec, h_fox)
    return (y_prompt, y_sample, sb_k_prompt, sb_v_prompt, sb_k_sample, sb_v_sample,
            ssm_re_prompt, ssm_im_prompt, ssm_re_sample, ssm_im_sample,
            fox_k_prompt, fox_v_prompt, fox_logf_prompt, fox_k_sample, fox_v_sample, fox_logf_sample)
```

```python
import functools

import jax
import jax.numpy as jnp
from jax import lax
from jax.experimental import pallas as pl
from jax.experimental.pallas import tpu as pltpu

F32 = jnp.float32
BF16 = jnp.bfloat16

HEAD_DIM = 128
SSM_GROUP = 16
P_SSM = 64
RMS_EPS = 1e-6
PAGE = 128
HEADS_PER_BLOCK = 8
SB_PAGES_PER_STEP = 16
FOX_PAGES_PER_STEP = 8
PROJ_ROWS = 1024
PROJ_COLS = 1024
SCAN_ROWS = 8
NEG_BIG = -1e30
MIB = 1024 * 1024


def _cparams(sem, vmem_mib):
    return pltpu.CompilerParams(dimension_semantics=sem, vmem_limit_bytes=vmem_mib * MIB)


def _split2(x):
    hi = x.astype(BF16)
    lo = (x - hi.astype(F32)).astype(BF16)
    return hi, lo


def _split3(x):
    hi = x.astype(BF16)
    r = x - hi.astype(F32)
    mid = r.astype(BF16)
    lo = (r - mid.astype(F32)).astype(BF16)
    return hi, mid, lo


def _dot(a, b):
    return jnp.dot(a, b, preferred_element_type=F32)


def _dot_nt(a, b):
    return lax.dot_general(a, b, (((1,), (1,)), ((), ())), preferred_element_type=F32)


def _neg_softplus(z):
    return -(jnp.maximum(z, 0.0) + jnp.log1p(jnp.exp(-jnp.abs(z))))


def _tri(n):
    r = lax.broadcasted_iota(jnp.int32, (n, n), 0)
    c = lax.broadcasted_iota(jnp.int32, (n, n), 1)
    return (r > c).astype(BF16)


def _norm_proj_kernel(x_ref, g_ref, w_ref, o_ref, h_scr):
    @pl.when(pl.program_id(1) == 0)
    def _():
        x = x_ref[...]
        y = x * lax.rsqrt(jnp.mean(x * x, axis=-1, keepdims=True) + RMS_EPS)
        h_scr[...] = (y * g_ref[...]).astype(BF16)

    o_ref[...] = _dot(h_scr[...], w_ref[...])


def norm_proj(x, g, w, tm, tn):
    m, d = x.shape
    n = w.shape[1]
    return pl.pallas_call(
        _norm_proj_kernel,
        grid=(m // tm, n // tn),
        in_specs=[
            pl.BlockSpec((tm, d), lambda i, j: (i, 0)),
            pl.BlockSpec((1, d), lambda i, j: (0, 0)),
            pl.BlockSpec((d, tn), lambda i, j: (0, j)),
        ],
        out_specs=pl.BlockSpec((tm, tn), lambda i, j: (i, j)),
        out_shape=jax.ShapeDtypeStruct((m, n), F32),
        scratch_shapes=[pltpu.VMEM((tm, d), BF16)],
        compiler_params=_cparams(("parallel", "arbitrary"), 56),
        name="norm_proj",
    )(x, g.reshape(1, d), w)


def _logf_kernel(x_ref, g_ref, w_ref, b_ref, o_ref):
    x = x_ref[...]
    y = x * lax.rsqrt(jnp.mean(x * x, axis=-1, keepdims=True) + RMS_EPS)
    h = (y * g_ref[...]).astype(BF16)
    f = _dot(h, w_ref[...]) + b_ref[...]
    o_ref[...] = _neg_softplus(-f)


def fox_logf(x, g, w_pad, b_pad, tm):
    m, d = x.shape
    n = w_pad.shape[1]
    return pl.pallas_call(
        _logf_kernel,
        grid=(m // tm,),
        in_specs=[
            pl.BlockSpec((tm, d), lambda i: (i, 0)),
            pl.BlockSpec((1, d), lambda i: (0, 0)),
            pl.BlockSpec((d, n), lambda i: (0, 0)),
            pl.BlockSpec((1, n), lambda i: (0, 0)),
        ],
        out_specs=pl.BlockSpec((tm, n), lambda i: (i, 0)),
        out_shape=jax.ShapeDtypeStruct((m, n), F32),
        compiler_params=_cparams(("parallel",), 32),
        name="fox_logf",
    )(x, g.reshape(1, d), w_pad, b_pad)


def _cumsum_kernel(x_ref, o_ref, *, blk):
    n = x_ref.shape[0]
    r = lax.broadcasted_iota(jnp.int32, (blk, blk), 0)
    c = lax.broadcasted_iota(jnp.int32, (blk, blk), 1)
    low = (c <= r).astype(BF16)

    def body(i, car):
        off = pl.multiple_of(i * blk, blk)
        hi, mid, lo = _split3(x_ref[pl.ds(off, blk), :])
        cum = _dot(low, hi) + _dot(low, mid) + _dot(low, lo) + car
        o_ref[pl.ds(off, blk), :] = cum
        return cum[blk - 1:blk, :]

    lax.fori_loop(0, n // blk, body, jnp.zeros((1, x_ref.shape[1]), F32))


def seq_cumsum(x):
    b, n, w = x.shape
    return pl.pallas_call(
        functools.partial(_cumsum_kernel, blk=128),
        grid=(b,),
        in_specs=[pl.BlockSpec((None, n, w), lambda i: (i, 0, 0))],
        out_specs=pl.BlockSpec((None, n, w), lambda i: (i, 0, 0)),
        out_shape=jax.ShapeDtypeStruct((b, n, w), F32),
        compiler_params=_cparams(("parallel",), 32),
        name="seq_cumsum",
    )(x)


def _sb_prompt_kernel(q_ref, k_ref, v_ref, o_ref, qb_scr, car_scr, *, tq, tk, hb):
    i = pl.program_id(2)
    qb_scr[...] = (q_ref[...] * (HEAD_DIM ** -0.5)).astype(BF16)
    o_ref[...] = jnp.zeros_like(o_ref)
    car_scr[...] = jnp.zeros_like(car_scr)
    t_pos = i * tq + lax.broadcasted_iota(jnp.int32, (tq, tk), 0)
    s_loc = lax.broadcasted_iota(jnp.int32, (tq, tk), 1)
    later = jnp.concatenate([_tri(tk), jnp.ones((tk, tk), BF16)], axis=1)
    nblk = (i + 1) * (tq // tk)

    def make_body(on_diagonal):
        def body(n, carry):
            off = pl.multiple_of((nblk - 1 - n) * tk, tk)
            before = (off + s_loc) < t_pos
            heads = [slice(h * HEAD_DIM, (h + 1) * HEAD_DIM) for h in range(hb)]
            zs = [_dot_nt(qb_scr[:, hc], k_ref[pl.ds(off, tk), hc].astype(BF16)) for hc in heads]
            lfs = [_neg_softplus(z) for z in zs]
            if on_diagonal:
                lfs = [jnp.where(before, lf, 0.0) for lf in lfs]
            sums = []
            for lf in lfs:
                hi, lo = _split2(lf)
                sums.append(_dot(hi, later) + _dot(lo, later))
            for h, hc in enumerate(heads):
                w = jnp.exp(zs[h] + lfs[h] + (sums[h][:, :tk] + car_scr[h]))
                if on_diagonal:
                    w = jnp.where(before, w, 0.0)
                o_ref[:, hc] += _dot(w.astype(BF16), v_ref[pl.ds(off, tk), hc].astype(BF16))
                car_scr[h] += sums[h][:, tk:]
            return carry
        return body

    lax.fori_loop(0, tq // tk, make_body(True), 0)
    lax.fori_loop(tq // tk, nblk, make_body(False), 0)


def sb_attn_prompt(proj, nb, seq, nh, tq=256, tk=128, hb=4):
    nq = seq // tq
    ng = nh // hb
    wb = hb * HEAD_DIM
    return pl.pallas_call(
        functools.partial(_sb_prompt_kernel, tq=tq, tk=tk, hb=hb),
        grid=(nb, ng, nq),
        in_specs=[
            pl.BlockSpec((tq, wb), lambda b, g, i: (b * nq + i, g)),
            pl.BlockSpec((seq, wb), lambda b, g, i: (b, ng + g)),
            pl.BlockSpec((seq, wb), lambda b, g, i: (b, 2 * ng + g)),
        ],
        out_specs=pl.BlockSpec((tq, wb), lambda b, g, i: (b * nq + i, g)),
        out_shape=jax.ShapeDtypeStruct((nb * seq, nh * HEAD_DIM), F32),
        scratch_shapes=[pltpu.VMEM((tq, wb), BF16), pltpu.VMEM((hb, tq, tk), F32)],
        compiler_params=_cparams(("parallel", "parallel", "arbitrary"), 40),
        name="sb_attn_prompt",
    )(proj, proj, proj)


def _fox_prompt_kernel(q_ref, k_ref, v_ref, ccol_ref, crow_ref, o_ref, qb_scr, ct_scr, m_scr, l_scr,
                       *, tq, tk, hb):
    i = pl.program_id(2)
    reps = tk // HEAD_DIM
    qb_scr[...] = (q_ref[...] * (HEAD_DIM ** -0.5)).astype(BF16)
    o_ref[...] = jnp.zeros_like(o_ref)
    m_scr[...] = jnp.full(m_scr.shape, NEG_BIG, F32)
    l_scr[...] = jnp.zeros_like(l_scr)
    for h in range(hb):
        ct_scr[h] = jnp.broadcast_to(ccol_ref[h], (tq, HEAD_DIM))
    t_pos = i * tq + lax.broadcasted_iota(jnp.int32, (tq, tk), 0)
    s_loc = lax.broadcasted_iota(jnp.int32, (tq, tk), 1)
    ones = jnp.ones((tk, HEAD_DIM), BF16)
    nblk = (i + 1) * (tq // tk)

    def wide(x):
        return jnp.concatenate([x] * reps, axis=1)

    def make_body(on_diagonal):
        def body(j, carry):
            off = pl.multiple_of(j * tk, tk)
            heads = [slice(h * HEAD_DIM, (h + 1) * HEAD_DIM) for h in range(hb)]
            zs = [_dot_nt(qb_scr[:, hc], k_ref[pl.ds(off, tk), hc].astype(BF16)) for hc in heads]
            ps = []
            alphas = []
            for h in range(hb):
                cs = crow_ref[h, :, pl.ds(off, tk)]
                s = zs[h] + (wide(ct_scr[h]) - cs)
                if on_diagonal:
                    s = jnp.where((off + s_loc) <= t_pos, s, NEG_BIG)
                m = m_scr[h]
                m_new = jnp.maximum(m, jnp.max(s, axis=1, keepdims=True))
                alphas.append(jnp.exp(m - m_new))
                ps.append(jnp.exp(s - wide(m_new)).astype(BF16))
                m_scr[h] = m_new
            for h, hc in enumerate(heads):
                l_scr[h] = alphas[h] * l_scr[h] + _dot(ps[h], ones)
                pv = _dot(ps[h], v_ref[pl.ds(off, tk), hc].astype(BF16))
                o_ref[:, hc] = alphas[h] * o_ref[:, hc] + pv
            return carry
        return body

    lax.fori_loop(0, nblk - tq // tk, make_body(False), 0)
    lax.fori_loop(nblk - tq // tk, nblk, make_body(True), 0)
    for h in range(hb):
        cols = slice(h * HEAD_DIM, (h + 1) * HEAD_DIM)
        o_ref[:, cols] = o_ref[:, cols] / l_scr[h]


def fox_attn_prompt(proj, cum_col, cum_row, nb, seq, nh, tq=256, tk=256, hb=4):
    nq = seq // tq
    ng = nh // hb
    wb = hb * HEAD_DIM
    return pl.pallas_call(
        functools.partial(_fox_prompt_kernel, tq=tq, tk=tk, hb=hb),
        grid=(nb, ng, nq),
        in_specs=[
            pl.BlockSpec((tq, wb), lambda b, g, i: (b * nq + i, g)),
            pl.BlockSpec((seq, wb), lambda b, g, i: (b, ng + g)),
            pl.BlockSpec((seq, wb), lambda b, g, i: (b, 2 * ng + g)),
            pl.BlockSpec((None, hb, tq, 1), lambda b, g, i: (b, g, i, 0)),
            pl.BlockSpec((None, hb, 1, seq), lambda b, g, i: (b, g, 0, 0)),
        ],
        out_specs=pl.BlockSpec((tq, wb), lambda b, g, i: (b * nq + i, g)),
        out_shape=jax.ShapeDtypeStruct((nb * seq, nh * HEAD_DIM), F32),
        scratch_shapes=[pltpu.VMEM((tq, wb), BF16), pltpu.VMEM((hb, tq, HEAD_DIM), F32),
                        pltpu.VMEM((hb, tq, HEAD_DIM), F32), pltpu.VMEM((hb, tq, HEAD_DIM), F32)],
        compiler_params=_cparams(("parallel", "parallel", "arbitrary"), 40),
        name="fox_attn_prompt",
    )(proj, proj, proj, cum_col, cum_row)


GROUP_LANES = HEADS_PER_BLOCK * HEAD_DIM


def _head_select():
    lane_head = lax.broadcasted_iota(jnp.int32, (HEADS_PER_BLOCK, GROUP_LANES), 1) // HEAD_DIM
    return lane_head == lax.broadcasted_iota(jnp.int32, (HEADS_PER_BLOCK, GROUP_LANES), 0)


def _block_diag_queries(q8, nh, nq):
    sel = _head_select()
    rows = []
    for t in range(nq):
        for g in range(nh // HEADS_PER_BLOCK):
            qg = q8[t:t + 1, g * GROUP_LANES:(g + 1) * GROUP_LANES]
            rows.append(jnp.where(sel, jnp.broadcast_to(qg, (HEADS_PER_BLOCK, GROUP_LANES)), 0.0))
    rows.append(jnp.zeros((PAGE - nq * nh, GROUP_LANES), F32))
    return jnp.concatenate(rows, axis=0)


def _gather_heads(acc, nh, nq):
    sel = _head_select()
    rows = []
    for t in range(nq):
        groups = []
        for g in range(nh // HEADS_PER_BLOCK):
            r0 = t * nh + g * HEADS_PER_BLOCK
            groups.append(jnp.sum(jnp.where(sel, acc[r0:r0 + HEADS_PER_BLOCK, :], 0.0),
                                  axis=0, keepdims=True))
        rows.append(jnp.concatenate(groups, axis=1))
    rows.append(jnp.zeros((8 - nq, nh * HEAD_DIM), F32))
    return jnp.concatenate(rows, axis=0)


def _new_rows_page(rows8, nh):
    parts = [rows8[:, g * GROUP_LANES:(g + 1) * GROUP_LANES] for g in range(nh // HEADS_PER_BLOCK)]
    parts.append(jnp.zeros((PAGE - 8 * len(parts), GROUP_LANES), F32))
    return jnp.concatenate(parts, axis=0).astype(BF16)


def _load_page(ref, nh):
    n_rows = PAGE * nh // HEADS_PER_BLOCK
    parts = [ref[pl.ds(j, n_rows, stride=HEADS_PER_BLOCK), :] for j in range(HEADS_PER_BLOCK)]
    return jnp.concatenate(parts, axis=1).astype(BF16)


def _suffix_sum(x):
    n = x.shape[0]
    row = lax.broadcasted_iota(jnp.int32, x.shape, 0)
    d = 1
    while d < n:
        x = x + jnp.where(row < n - d, pltpu.roll(x, n - d, axis=0), 0.0)
        d *= 2
    return x


def _sb_decode_kernel(pt_ref, q_ref, kn_ref, vn_ref, *refs, nh, nq, pps):
    k_refs = refs[:pps]
    v_refs = refs[pps:2 * pps]
    o_ref = refs[2 * pps]
    qbd_scr, car_scr, acc_scr = refs[2 * pps + 1:]
    s = pl.program_id(1)
    nl = nq * nh

    @pl.when(s == 0)
    def _():
        qbd = _block_diag_queries(q_ref[...] * (HEAD_DIM ** -0.5), nh, nq).T
        qbd_scr[...] = qbd.astype(BF16)
        key = lax.broadcasted_iota(jnp.int32, (PAGE, PAGE), 0)
        vis = key < lax.broadcasted_iota(jnp.int32, (PAGE, PAGE), 1) // nh
        z = _dot(_new_rows_page(kn_ref[...], nh), qbd_scr[...])
        suf = _suffix_sum(jnp.where(vis, _neg_softplus(z), 0.0))
        w = jnp.where(vis, jnp.exp(z + suf), 0.0)
        car_scr[...] = suf[0:1, :]
        acc_scr[...] = _dot(w.T[:nl, :].astype(BF16), _new_rows_page(vn_ref[...], nh))

    def scores(r):
        return _dot(_load_page(k_refs[r], nh), qbd_scr[...])

    ahead = 2
    zs = [scores(r) for r in range(min(ahead, pps))]
    car = car_scr[...]
    acc = acc_scr[...]
    for r in range(pps):
        suf = _suffix_sum(_neg_softplus(zs[r]))
        w = jnp.exp(zs[r] + suf + car)
        car = car + suf[0:1, :]
        if r + ahead < pps:
            zs.append(scores(r + ahead))
        acc = acc + _dot(w.T[:nl, :].astype(BF16), _load_page(v_refs[r], nh))
    car_scr[...] = car
    acc_scr[...] = acc

    @pl.when(s == pl.num_programs(1) - 1)
    def _():
        o_ref[...] = _gather_heads(acc_scr[...], nh, nq)


def _page_specs(rows, n_pages, pps):
    specs = []
    for r in range(pps):
        def imap(b, s, pt, r=r):
            return (pt[b, n_pages - 1 - (s * pps + r)], 0, 0)
        specs.append(pl.BlockSpec((None, rows, HEAD_DIM), imap))
    return specs


def sb_attn_decode(page_table, q8, kn8, vn8, pool_k, pool_v, nh, nq, pps):
    assert nh == HEADS_PER_BLOCK
    nb, n_pages = page_table.shape
    w = nh * HEAD_DIM
    row_spec = pl.BlockSpec((None, 8, w), lambda b, s, pt: (b, 0, 0))
    grid_spec = pltpu.PrefetchScalarGridSpec(
        num_scalar_prefetch=1,
        grid=(nb, n_pages // pps),
        in_specs=[row_spec, row_spec, row_spec]
        + _page_specs(PAGE * nh, n_pages, pps) + _page_specs(PAGE * nh, n_pages, pps),
        out_specs=row_spec,
        scratch_shapes=[
            pltpu.VMEM((GROUP_LANES, PAGE), BF16),
            pltpu.VMEM((1, PAGE), F32),
            pltpu.VMEM((nq * nh, GROUP_LANES), F32),
        ],
    )
    return pl.pallas_call(
        functools.partial(_sb_decode_kernel, nh=nh, nq=nq, pps=pps),
        grid_spec=grid_spec,
        out_shape=jax.ShapeDtypeStruct((nb, 8, w), F32),
        compiler_params=_cparams(("arbitrary", "arbitrary"), 56),
        name="sb_attn_decode",
    )(page_table, q8, kn8, vn8, *([pool_k] * pps), *([pool_v] * pps))


def _tile_lanes(x16, nh, nq):
    out = x16
    for t in range(1, nq):
        out = out + pltpu.roll(x16, t * nh, axis=1)
    return out


def _fox_decode_kernel(pt_ref, q_ref, kn_ref, vn_ref, lfn_ref, *refs, nh, nq, pps):
    ng = nh // HEADS_PER_BLOCK
    k_refs = refs[:pps]
    v_refs = refs[pps:2 * pps]
    lf_refs = refs[2 * pps:3 * pps]
    o_ref = refs[3 * pps]
    qbd_scr, lfpad_scr, bias_scr, car_scr, pre_scr, m_scr, l_scr, acc_scr = refs[3 * pps + 1:]
    s = pl.program_id(1)
    nl = nq * nh
    n_rows = PAGE * ng
    row_group = lax.broadcasted_iota(jnp.int32, (n_rows, PAGE), 0) % ng
    lane_group = (lax.broadcasted_iota(jnp.int32, (n_rows, PAGE), 1) % nh) // HEADS_PER_BLOCK
    valid = row_group == lane_group

    def to_col(row):
        return jnp.broadcast_to(row, (PAGE, PAGE)).T[:nl, 0:1]

    def to_rows(p):
        blocks = [p[i * PAGE:(i + 1) * PAGE, :].T for i in range(p.shape[0] // PAGE)]
        return jnp.concatenate(blocks, axis=1)[:nl, :].astype(BF16)

    @pl.when(s == 0)
    def _():
        qbd = _block_diag_queries(q_ref[...] * (HEAD_DIM ** -0.5), nh, nq).T
        qbd_scr[...] = qbd.astype(BF16)
        lfpad_scr[...] = jnp.zeros((PAGE, PAGE), F32)
        lfn = lfn_ref[...]
        key8 = lax.broadcasted_iota(jnp.int32, (8, PAGE), 0)
        qry8 = lax.broadcasted_iota(jnp.int32, (8, PAGE), 1) // nh
        pre_q = jnp.sum(jnp.where(key8 <= qry8, lfn, 0.0), axis=0, keepdims=True)
        run = jnp.zeros((1, PAGE), F32)
        pre_rows = []
        for j in range(8):
            run = run + lfn[j:j + 1, :]
            pre_rows.append(run)
        pre_k = jnp.concatenate(pre_rows, axis=0)
        nr = 8 * ng
        rown = lax.broadcasted_iota(jnp.int32, (nr, PAGE), 0)
        lanen = lax.broadcasted_iota(jnp.int32, (nr, PAGE), 1)
        seen = ((rown // 8) == (lanen % nh) // HEADS_PER_BLOCK) & ((rown % 8) <= lanen // nh)
        z = _dot(_new_rows_page(kn_ref[...], nh), qbd_scr[...])[:nr, :]
        sc = jnp.where(seen, z + jnp.concatenate([pre_q - pre_k] * ng, axis=0), NEG_BIG)
        m = jnp.max(sc, axis=0, keepdims=True)
        p = jnp.exp(sc - m)
        pt = to_rows(jnp.concatenate([p, jnp.zeros((PAGE - nr, PAGE), F32)], axis=0))
        acc_scr[...] = _dot(pt, _new_rows_page(vn_ref[...], nh))
        l_scr[...] = jnp.sum(p, axis=0, keepdims=True)
        m_scr[...] = m
        pre_scr[...] = pre_q
        car_scr[...] = jnp.zeros((1, PAGE), F32)

    zs = [_dot(_load_page(k_refs[r], nh), qbd_scr[...]) for r in range(pps)]
    car = car_scr[...]
    pre_q = pre_scr[...]
    scores = []
    for r in range(pps):
        lfpad_scr[:, 0:nh] = lf_refs[r][...]
        lf = _tile_lanes(lfpad_scr[...], nh, nq)
        suf = _suffix_sum(lf)
        bias = (suf - lf) + car + pre_q
        for g in range(ng):
            bias_scr[pl.ds(g, PAGE, stride=ng), :] = bias
        scores.append(jnp.where(valid, zs[r] + bias_scr[...], NEG_BIG))
        car = car + suf[0:1, :]
    car_scr[...] = car

    m_old = m_scr[...]
    m_new = m_old
    for sc in scores:
        m_new = jnp.maximum(m_new, jnp.max(sc, axis=0, keepdims=True))
    alpha = jnp.exp(m_old - m_new)
    l = alpha * l_scr[...]
    acc = acc_scr[...] * to_col(alpha)
    for r in range(pps):
        p = jnp.exp(scores[r] - m_new)
        l = l + jnp.sum(p, axis=0, keepdims=True)
        acc = acc + _dot(to_rows(p), _load_page(v_refs[r], nh))
    acc_scr[...] = acc
    l_scr[...] = l
    m_scr[...] = m_new

    @pl.when(s == pl.num_programs(1) - 1)
    def _():
        o_ref[...] = _gather_heads(acc_scr[...] / to_col(l_scr[...]), nh, nq)


def fox_attn_decode(page_table, q8, kn8, vn8, lfn8, pool_k, pool_v, pool_lf, nh, nq, pps):
    nb, n_pages = page_table.shape
    w = nh * HEAD_DIM
    ng = nh // HEADS_PER_BLOCK
    row_spec = pl.BlockSpec((None, 8, w), lambda b, s, pt: (b, 0, 0))
    lf_specs = []
    for r in range(pps):
        def imap(b, s, pt, r=r):
            return (pt[b, n_pages - 1 - (s * pps + r)], 0, 0)
        lf_specs.append(pl.BlockSpec((None, PAGE, nh), imap))
    grid_spec = pltpu.PrefetchScalarGridSpec(
        num_scalar_prefetch=1,
        grid=(nb, n_pages // pps),
        in_specs=[row_spec, row_spec, row_spec,
                  pl.BlockSpec((None, 8, PAGE), lambda b, s, pt: (b, 0, 0))]
        + _page_specs(PAGE * nh, n_pages, pps) + _page_specs(PAGE * nh, n_pages, pps) + lf_specs,
        out_specs=row_spec,
        scratch_shapes=[
            pltpu.VMEM((GROUP_LANES, PAGE), BF16),
            pltpu.VMEM((PAGE, PAGE), F32),
            pltpu.VMEM((PAGE * ng, PAGE), F32),
            pltpu.VMEM((1, PAGE), F32),
            pltpu.VMEM((1, PAGE), F32),
            pltpu.VMEM((1, PAGE), F32),
            pltpu.VMEM((1, PAGE), F32),
            pltpu.VMEM((nq * nh, GROUP_LANES), F32),
        ],
    )
    return pl.pallas_call(
        functools.partial(_fox_decode_kernel, nh=nh, nq=nq, pps=pps),
        grid_spec=grid_spec,
        out_shape=jax.ShapeDtypeStruct((nb, 8, w), F32),
        compiler_params=_cparams(("arbitrary", "arbitrary"), 56),
        name="fox_attn_decode",
    )(page_table, q8, kn8, vn8, lfn8, *([pool_k] * pps), *([pool_v] * pps), *([pool_lf] * pps))


def _ssm_prep_kernel(are_ref, aim_ref, ldt_ref, bre_ref, bim_ref,
                     pwr_ref, pwi_ref, bbr_ref, bbi_ref):
    ar = are_ref[...]
    ai = aim_ref[...]
    dt = jnp.exp(ldt_ref[...])
    mag = jnp.exp(ar * dt)
    abr = mag * jnp.cos(ai * dt)
    abi = mag * jnp.sin(ai * dt)
    den = ar * ar + ai * ai
    nr = abr - 1.0
    ni = abi
    cr = (nr * ar + ni * ai) / den
    ci = (ni * ar - nr * ai) / den
    br = bre_ref[...]
    bi = bim_ref[...]
    bbr_ref[...] = cr * br - ci * bi
    bbi_ref[...] = cr * bi + ci * br
    pr, pi = abr, abi
    pwr_ref[0] = pr
    pwi_ref[0] = pi
    for k in range(1, SCAN_ROWS):
        pr, pi = pr * abr - pi * abi, pr * abi + pi * abr
        pwr_ref[k] = pr
        pwi_ref[k] = pi


def ssm_prep(a_re, a_im, log_dt, b_re_t, b_im_t):
    g, p = a_re.shape
    c = b_re_t.shape[1]
    return pl.pallas_call(
        _ssm_prep_kernel,
        out_shape=(jax.ShapeDtypeStruct((SCAN_ROWS, g, 1, p), F32),
                   jax.ShapeDtypeStruct((SCAN_ROWS, g, 1, p), F32),
                   jax.ShapeDtypeStruct((g, c, p), F32), jax.ShapeDtypeStruct((g, c, p), F32)),
        name="ssm_prep",
    )(a_re.reshape(g, 1, p), a_im.reshape(g, 1, p), log_dt.reshape(g, 1, 1), b_re_t, b_im_t)


def _ssm_prompt_kernel(u_ref, wbr_ref, wbi_ref, wcr_ref, wci_ref, d_ref, pwr_ref, pwi_ref,
                       y_ref, hr_ref, hi_ref, xr_scr, xi_scr, sr_scr, si_scr, *, tt):
    i = pl.program_id(2)
    lanes = xr_scr.shape[1]

    @pl.when(i == 0)
    def _():
        sr_scr[...] = jnp.zeros_like(sr_scr)
        si_scr[...] = jnp.zeros_like(si_scr)

    u = u_ref[...]
    ub = u.astype(BF16)
    xr_scr[...] = _dot(ub, wbr_ref[...])
    xi_scr[...] = _dot(ub, wbi_ref[...])

    pw_r = pwr_ref[...]
    pw_i = pwi_ref[...]
    row = lax.broadcasted_iota(jnp.int32, (SCAN_ROWS, lanes), 0)

    def tile(r, carry):
        h_r, h_i = carry
        off = pl.multiple_of(r * SCAN_ROWS, SCAN_ROWS)
        x_r = xr_scr[pl.ds(off, SCAN_ROWS), :]
        x_i = xi_scr[pl.ds(off, SCAN_ROWS), :]
        for sh in (1, 2, 4):
            a_r = pw_r[sh - 1:sh, :]
            a_i = pw_i[sh - 1:sh, :]
            p_r = jnp.where(row >= sh, pltpu.roll(x_r, sh, axis=0), 0.0)
            p_i = jnp.where(row >= sh, pltpu.roll(x_i, sh, axis=0), 0.0)
            x_r, x_i = x_r + (a_r * p_r - a_i * p_i), x_i + (a_r * p_i + a_i * p_r)
        x_r, x_i = x_r + (pw_r * h_r - pw_i * h_i), x_i + (pw_r * h_i + pw_i * h_r)
        xr_scr[pl.ds(off, SCAN_ROWS), :] = x_r
        xi_scr[pl.ds(off, SCAN_ROWS), :] = x_i
        return x_r[SCAN_ROWS - 1:SCAN_ROWS, :], x_i[SCAN_ROWS - 1:SCAN_ROWS, :]

    h_r, h_i = lax.fori_loop(0, tt // SCAN_ROWS, tile, (sr_scr[...], si_scr[...]))
    sr_scr[...] = h_r
    si_scr[...] = h_i
    hr_ref[...] = h_r
    hi_ref[...] = h_i
    y_ref[...] = (_dot(xr_scr[...].astype(BF16), wcr_ref[...])
                  - _dot(xi_scr[...].astype(BF16), wci_ref[...]) + d_ref[...] * u)


def ssm_prompt(proj, u_col, nb, seq, wbr, wbi, wcr, wci, d, pw_r, pw_i, tt=512):
    nchunk, cw, lanes = wbr.shape
    nt = seq // tt
    ucb = u_col * (nchunk)
    return pl.pallas_call(
        functools.partial(_ssm_prompt_kernel, tt=tt),
        grid=(nb, nchunk, nt),
        in_specs=[
            pl.BlockSpec((tt, cw), lambda b, j, i: (b * nt + i, ucb + j)),
            pl.BlockSpec((None, cw, lanes), lambda b, j, i: (j, 0, 0)),
            pl.BlockSpec((None, cw, lanes), lambda b, j, i: (j, 0, 0)),
            pl.BlockSpec((None, lanes, cw), lambda b, j, i: (j, 0, 0)),
            pl.BlockSpec((None, lanes, cw), lambda b, j, i: (j, 0, 0)),
            pl.BlockSpec((1, cw), lambda b, j, i: (0, j)),
            pl.BlockSpec((SCAN_ROWS, lanes), lambda b, j, i: (0, j)),
            pl.BlockSpec((SCAN_ROWS, lanes), lambda b, j, i: (0, j)),
        ],
        out_specs=(
            pl.BlockSpec((tt, cw), lambda b, j, i: (b * nt + i, j)),
            pl.BlockSpec((None, 1, lanes), lambda b, j, i: (b, 0, j)),
            pl.BlockSpec((None, 1, lanes), lambda b, j, i: (b, 0, j)),
        ),
        out_shape=(jax.ShapeDtypeStruct((nb * seq, nchunk * cw), F32),
                   jax.ShapeDtypeStruct((nb, 1, nchunk * lanes), F32),
                   jax.ShapeDtypeStruct((nb, 1, nchunk * lanes), F32)),
        scratch_shapes=[pltpu.VMEM((tt, lanes), F32), pltpu.VMEM((tt, lanes), F32),
                        pltpu.VMEM((1, lanes), F32), pltpu.VMEM((1, lanes), F32)],
        compiler_params=_cparams(("parallel", "parallel", "arbitrary"), 32),
        name="ssm_prompt",
    )(proj, wbr, wbi, wcr, wci, d, pw_r, pw_i)


def _ssm_sample_kernel(u_ref, h0r_ref, h0i_ref, wbr_ref, wbi_ref, wcr_ref, wci_ref, d_ref,
                       pwr_ref, pwi_ref, y_ref, hr_ref, hi_ref, *, steps):
    a_r = pwr_ref[0:1, :]
    a_i = pwi_ref[0:1, :]
    h_r = h0r_ref[...]
    h_i = h0i_ref[...]
    for t in range(steps):
        u = u_ref[t]
        ub = u.astype(BF16)
        h_r, h_i = (a_r * h_r - a_i * h_i + _dot(ub, wbr_ref[...]),
                    a_r * h_i + a_i * h_r + _dot(ub, wbi_ref[...]))
        y_ref[t] = (_dot(h_r.astype(BF16), wcr_ref[...]) - _dot(h_i.astype(BF16), wci_ref[...])
                    + d_ref[...] * u)
    hr_ref[...] = h_r
    hi_ref[...] = h_i


def ssm_sample(u_t, h0_r, h0_i, wbr, wbi, wcr, wci, d, pw_r, pw_i):
    steps, nb, _ = u_t.shape
    nchunk, cw, lanes = wbr.shape
    return pl.pallas_call(
        functools.partial(_ssm_sample_kernel, steps=steps),
        grid=(nchunk,),
        in_specs=[
            pl.BlockSpec((steps, nb, cw), lambda j: (0, 0, j)),
            pl.BlockSpec((nb, lanes), lambda j: (0, j)),
            pl.BlockSpec((nb, lanes), lambda j: (0, j)),
            pl.BlockSpec((None, cw, lanes), lambda j: (j, 0, 0)),
            pl.BlockSpec((None, cw, lanes), lambda j: (j, 0, 0)),
            pl.BlockSpec((None, lanes, cw), lambda j: (j, 0, 0)),
            pl.BlockSpec((None, lanes, cw), lambda j: (j, 0, 0)),
            pl.BlockSpec((1, cw), lambda j: (0, j)),
            pl.BlockSpec((SCAN_ROWS, lanes), lambda j: (0, j)),
            pl.BlockSpec((SCAN_ROWS, lanes), lambda j: (0, j)),
        ],
        out_specs=(
            pl.BlockSpec((steps, nb, cw), lambda j: (0, 0, j)),
            pl.BlockSpec((nb, lanes), lambda j: (0, j)),
            pl.BlockSpec((nb, lanes), lambda j: (0, j)),
        ),
        out_shape=(jax.ShapeDtypeStruct((steps, nb, nchunk * cw), F32),
                   jax.ShapeDtypeStruct((nb, nchunk * lanes), F32),
                   jax.ShapeDtypeStruct((nb, nchunk * lanes), F32)),
        compiler_params=_cparams(("parallel",), 32),
        name="ssm_sample",
    )(u_t, h0_r, h0_i, wbr, wbi, wcr, wci, d, pw_r, pw_i)


def _even_out_kernel(oa_ref, ga_ref, yb_ref, gb_ref, x_ref, wg_ref, bg_ref, wo_ref, o_ref):
    half = oa_ref.shape[1]
    ya = oa_ref[...] * jax.nn.silu(ga_ref[...])
    yb = jax.nn.gelu(yb_ref[...])
    yb = yb * jax.nn.sigmoid(_dot(yb.astype(BF16), wg_ref[...]) + bg_ref[...])
    yb = yb * jax.nn.silu(gb_ref[...])
    o_ref[...] = (x_ref[...] + _dot(ya.astype(BF16), wo_ref[0:half, :])
                  + _dot(yb.astype(BF16), wo_ref[half:, :]))


def even_out(o_a, y_b, proj, x, w_glu, b_glu, w_out, tm=256):
    m, half = o_a.shape
    tm = min(tm, m)
    d = x.shape[1]
    return pl.pallas_call(
        _even_out_kernel,
        grid=(m // tm,),
        in_specs=[
            pl.BlockSpec((tm, half), lambda i: (i, 0)),
            pl.BlockSpec((tm, half), lambda i: (i, 3)),
            pl.BlockSpec((tm, half), lambda i: (i, 0)),
            pl.BlockSpec((tm, half), lambda i: (i, 5)),
            pl.BlockSpec((tm, d), lambda i: (i, 0)),
            pl.BlockSpec((half, half), lambda i: (0, 0)),
            pl.BlockSpec((1, half), lambda i: (0, 0)),
            pl.BlockSpec((2 * half, d), lambda i: (0, 0)),
        ],
        out_specs=pl.BlockSpec((tm, d), lambda i: (i, 0)),
        out_shape=jax.ShapeDtypeStruct((m, d), F32),
        compiler_params=_cparams(("parallel",), 48),
        name="even_out",
    )(o_a, proj, y_b, proj, x, w_glu, b_glu, w_out)


def _odd_out_kernel(o_ref_in, g_ref, x_ref, wo_ref, gf_ref, y_ref):
    y = o_ref_in[...] * jax.nn.silu(g_ref[...])
    r = x_ref[...] + _dot(y.astype(BF16), wo_ref[...])
    n = r * lax.rsqrt(jnp.mean(r * r, axis=-1, keepdims=True) + RMS_EPS)
    y_ref[...] = n * gf_ref[...]


def odd_out(o, proj, x, w_out, g_final, tm=256):
    m, d = x.shape
    tm = min(tm, m)
    w = o.shape[1]
    return pl.pallas_call(
        _odd_out_kernel,
        grid=(m // tm,),
        in_specs=[
            pl.BlockSpec((tm, w), lambda i: (i, 0)),
            pl.BlockSpec((tm, w), lambda i: (i, 3)),
            pl.BlockSpec((tm, d), lambda i: (i, 0)),
            pl.BlockSpec((w, d), lambda i: (0, 0)),
            pl.BlockSpec((1, d), lambda i: (0, 0)),
        ],
        out_specs=pl.BlockSpec((tm, d), lambda i: (i, 0)),
        out_shape=jax.ShapeDtypeStruct((m, d), F32),
        compiler_params=_cparams(("parallel",), 48),
        name="odd_out",
    )(o, proj, x, w_out, g_final.reshape(1, d))


def _pad_rows8(x):
    return jnp.pad(x, ((0, 0), (0, 8 - x.shape[1]), (0, 0)))


def _block_diag(x):
    nc, pc, a, b = x.shape
    eye = jnp.eye(pc, dtype=x.dtype)
    return jnp.einsum("jgab,gh->jgahb", x, eye).reshape(nc, pc * a, pc * b)


def kernel(x_prompt, x_sample, cache_sb_k, cache_sb_v, state_ssm_re, state_ssm_im, cache_fox_k, cache_fox_v, cache_fox_logf, page_table, norm_g, w_in_even, ssm_a_re, ssm_a_im, ssm_log_dt, ssm_b_re, ssm_b_im, ssm_c_re, ssm_c_im, ssm_d, w_glu, b_glu, w_out_even, w_in_odd, b_f, w_out_odd, norm_f_g):
    nbp, seq, d_model = x_prompt.shape
    nbs, dec = x_sample.shape[:2]
    mp = nbp * seq
    ms = nbs * dec
    w_sb = d_model // 2
    h_sb = w_sb // HEAD_DIM
    w_ssm = d_model - w_sb
    g_ssm = w_ssm // SSM_GROUP
    h_fox = d_model // HEAD_DIM
    n_pool = cache_sb_k.shape[1]
    groups_per_chunk = HEAD_DIM // SSM_GROUP
    n_chunks = g_ssm // groups_per_chunk

    xp = x_prompt.reshape(mp, d_model)
    xs = x_sample.reshape(ms, d_model)

    w_in0 = w_in_even[0].astype(BF16)
    proj0_p = norm_proj(xp, norm_g[0], w_in0, tm=PROJ_ROWS, tn=PROJ_COLS)
    proj0_s = norm_proj(xs, norm_g[0], w_in0, tm=ms, tn=PROJ_COLS)
    k0_s = proj0_s[:, w_sb:2 * w_sb]
    v0_s = proj0_s[:, 2 * w_sb:3 * w_sb]

    oa_p = sb_attn_prompt(proj0_p, nbp, seq, h_sb)
    q8 = _pad_rows8(proj0_s[:, :w_sb].reshape(nbs, dec, w_sb))
    kn8 = _pad_rows8(k0_s.reshape(nbs, dec, w_sb))
    vn8 = _pad_rows8(v0_s.reshape(nbs, dec, w_sb))
    oa_s = sb_attn_decode(page_table, q8, kn8, vn8,
                          cache_sb_k.reshape(n_pool, PAGE * h_sb, HEAD_DIM),
                          cache_sb_v.reshape(n_pool, PAGE * h_sb, HEAD_DIM),
                          h_sb, dec, SB_PAGES_PER_STEP)[:, :dec].reshape(ms, w_sb)

    b_re_t = jnp.swapaxes(ssm_b_re[0], 1, 2)
    b_im_t = jnp.swapaxes(ssm_b_im[0], 1, 2)
    pw_r, pw_i, bb_r, bb_i = ssm_prep(ssm_a_re[0], ssm_a_im[0], ssm_log_dt[0], b_re_t, b_im_t)
    n_state = g_ssm * P_SSM
    pw_r = pw_r.reshape(SCAN_ROWS, n_state)
    pw_i = pw_i.reshape(SCAN_ROWS, n_state)
    wbr = _block_diag(bb_r.reshape(n_chunks, groups_per_chunk, SSM_GROUP, P_SSM)).astype(BF16)
    wbi = _block_diag(bb_i.reshape(n_chunks, groups_per_chunk, SSM_GROUP, P_SSM)).astype(BF16)
    c_re_t = jnp.swapaxes(ssm_c_re[0], 1, 2)
    c_im_t = jnp.swapaxes(ssm_c_im[0], 1, 2)
    wcr = _block_diag(c_re_t.reshape(n_chunks, groups_per_chunk, P_SSM, SSM_GROUP)).astype(BF16)
    wci = _block_diag(c_im_t.reshape(n_chunks, groups_per_chunk, P_SSM, SSM_GROUP)).astype(BF16)
    d_row = ssm_d[0].reshape(1, w_ssm)

    yb_p, hr_p, hi_p = ssm_prompt(proj0_p, 4, nbp, seq, wbr, wbi, wcr, wci, d_row, pw_r, pw_i)
    u_s = jnp.swapaxes(proj0_s[:, 4 * w_sb:4 * w_sb + w_ssm].reshape(nbs, dec, w_ssm), 0, 1)
    yb_s, hr_s, hi_s = ssm_sample(u_s, state_ssm_re[0].reshape(nbs, n_state),
                                  state_ssm_im[0].reshape(nbs, n_state),
                                  wbr, wbi, wcr, wci, d_row, pw_r, pw_i)
    yb_s = jnp.swapaxes(yb_s, 0, 1).reshape(ms, w_ssm)

    w_glu0 = w_glu[0].astype(BF16)
    b_glu0 = b_glu[0].reshape(1, w_ssm)
    w_out0 = w_out_even[0].astype(BF16)
    x1_p = even_out(oa_p, yb_p, proj0_p, xp, w_glu0, b_glu0, w_out0)
    x1_s = even_out(oa_s, yb_s, proj0_s, xs, w_glu0, b_glu0, w_out0)

    w_main = w_in_odd[0][:, :4 * d_model].astype(BF16)
    w_f = jnp.pad(w_in_odd[0][:, 4 * d_model:], ((0, 0), (0, PAGE - h_fox))).astype(BF16)
    b_pad = jnp.pad(b_f[0], (0, PAGE - h_fox)).reshape(1, PAGE)
    proj1_p = norm_proj(x1_p, norm_g[1], w_main, tm=PROJ_ROWS, tn=PROJ_COLS)
    proj1_s = norm_proj(x1_s, norm_g[1], w_main, tm=ms, tn=PROJ_COLS)
    logf_p = fox_logf(x1_p, norm_g[1], w_f, b_pad, tm=512).reshape(nbp, seq, PAGE)
    logf_s = fox_logf(x1_s, norm_g[1], w_f, b_pad, tm=ms)[:, :h_fox].reshape(nbs, dec, h_fox)

    cum = seq_cumsum(logf_p)[:, :, :h_fox]
    cum_row = jnp.transpose(cum, (0, 2, 1))[:, :, None, :]
    cum_col = jnp.transpose(cum, (0, 2, 1))[:, :, :, None]
    o_p = fox_attn_prompt(proj1_p, cum_col, cum_row, nbp, seq, h_fox)

    k1_s = proj1_s[:, d_model:2 * d_model]
    v1_s = proj1_s[:, 2 * d_model:3 * d_model]
    q8 = _pad_rows8(proj1_s[:, :d_model].reshape(nbs, dec, d_model))
    kn8 = _pad_rows8(k1_s.reshape(nbs, dec, d_model))
    vn8 = _pad_rows8(v1_s.reshape(nbs, dec, d_model))
    lfn8 = _pad_rows8(jnp.pad(jnp.tile(logf_s, (1, 1, dec)), ((0, 0), (0, 0), (0, PAGE - dec * h_fox))))
    o_s = fox_attn_decode(page_table, q8, kn8, vn8, lfn8,
                          cache_fox_k.reshape(n_pool, PAGE * h_fox, HEAD_DIM),
                          cache_fox_v.reshape(n_pool, PAGE * h_fox, HEAD_DIM),
                          cache_fox_logf.reshape(n_pool, PAGE, h_fox),
                          h_fox, dec, FOX_PAGES_PER_STEP)[:, :dec].reshape(ms, d_model)

    w_out1 = w_out_odd[0].astype(BF16)
    y_prompt = odd_out(o_p, proj1_p, x1_p, w_out1, norm_f_g).reshape(nbp, seq, d_model)
    y_sample = odd_out(o_s, proj1_s, x1_s, w_out1, norm_f_g).reshape(nbs, dec, d_model)

    sb_k_prompt = proj0_p[:, w_sb:2 * w_sb].reshape(1, nbp, seq, h_sb, HEAD_DIM)
    sb_v_prompt = proj0_p[:, 2 * w_sb:3 * w_sb].reshape(1, nbp, seq, h_sb, HEAD_DIM)
    sb_k_sample = k0_s.reshape(1, nbs, dec, h_sb, HEAD_DIM)
    sb_v_sample = v0_s.reshape(1, nbs, dec, h_sb, HEAD_DIM)
    ssm_re_prompt = hr_p.reshape(1, nbp, g_ssm, P_SSM)
    ssm_im_prompt = hi_p.reshape(1, nbp, g_ssm, P_SSM)
    ssm_re_sample = hr_s.reshape(1, nbs, g_ssm, P_SSM)
    ssm_im_sample = hi_s.reshape(1, nbs, g_ssm, P_SSM)
    fox_k_prompt = proj1_p[:, d_model:2 * d_model].reshape(1, nbp, seq, h_fox, HEAD_DIM)
    fox_v_prompt = proj1_p[:, 2 * d_model:3 * d_model].reshape(1, nbp, seq, h_fox, HEAD_DIM)
    fox_logf_prompt = logf_p[:, :, :h_fox].reshape(1, nbp, seq, h_fox)
    fox_k_sample = k1_s.reshape(1, nbs, dec, h_fox, HEAD_DIM)
    fox_v_sample = v1_s.reshape(1, nbs, dec, h_fox, HEAD_DIM)
    fox_logf_sample = logf_s.reshape(1, nbs, dec, h_fox)
    return (y_prompt, y_sample, sb_k_prompt, sb_v_prompt, sb_k_sample, sb_v_sample,
            ssm_re_prompt, ssm_im_prompt, ssm_re_sample, ssm_im_sample,
            fox_k_prompt, fox_v_prompt, fox_logf_prompt, fox_k_sample, fox_v_sample, fox_logf_sample)
```

```python
import functools

import jax
import jax.numpy as jnp
from jax import lax
from jax.experimental import pallas as pl
from jax.experimental.pallas import tpu as pltpu

F32 = jnp.float32
BF16 = jnp.bfloat16

HEAD_DIM = 128
SSM_GROUP = 16
P_SSM = 64
RMS_EPS = 1e-6
PAGE = 128
HEADS_PER_BLOCK = 8
SB_PAGES_PER_STEP = 16
FOX_PAGES_PER_STEP = 8
PROJ_ROWS = 1024
PROJ_COLS = 1024
SCAN_ROWS = 8
CHUNK_T = 16
NEG_BIG = -1e30
MIB = 1024 * 1024


def _cparams(sem, vmem_mib):
    return pltpu.CompilerParams(dimension_semantics=sem, vmem_limit_bytes=vmem_mib * MIB)


def _split2(x):
    hi = x.astype(BF16)
    lo = (x - hi.astype(F32)).astype(BF16)
    return hi, lo


def _split3(x):
    hi = x.astype(BF16)
    r = x - hi.astype(F32)
    mid = r.astype(BF16)
    lo = (r - mid.astype(F32)).astype(BF16)
    return hi, mid, lo


def _dot(a, b):
    return jnp.dot(a, b, preferred_element_type=F32)


def _dot_nt(a, b):
    return lax.dot_general(a, b, (((1,), (1,)), ((), ())), preferred_element_type=F32)


def _neg_softplus(z):
    return -(jnp.maximum(z, 0.0) + jnp.log1p(jnp.exp(-jnp.abs(z))))


def _tri(n):
    r = lax.broadcasted_iota(jnp.int32, (n, n), 0)
    c = lax.broadcasted_iota(jnp.int32, (n, n), 1)
    return (r > c).astype(BF16)


def _norm_proj_kernel(x_ref, g_ref, w_ref, o_ref, h_scr):
    @pl.when(pl.program_id(1) == 0)
    def _():
        x = x_ref[...]
        y = x * lax.rsqrt(jnp.mean(x * x, axis=-1, keepdims=True) + RMS_EPS)
        h_scr[...] = (y * g_ref[...]).astype(BF16)

    o_ref[...] = _dot(h_scr[...], w_ref[...])


def norm_proj(x, g, w, tm, tn):
    m, d = x.shape
    n = w.shape[1] // tn * tn
    return pl.pallas_call(
        _norm_proj_kernel,
        grid=(m // tm, n // tn),
        in_specs=[
            pl.BlockSpec((tm, d), lambda i, j: (i, 0)),
            pl.BlockSpec((1, d), lambda i, j: (0, 0)),
            pl.BlockSpec((d, tn), lambda i, j: (0, j)),
        ],
        out_specs=pl.BlockSpec((tm, tn), lambda i, j: (i, j)),
        out_shape=jax.ShapeDtypeStruct((m, n), F32),
        scratch_shapes=[pltpu.VMEM((tm, d), BF16)],
        compiler_params=_cparams(("parallel", "arbitrary"), 56),
        name="norm_proj",
    )(x, g.reshape(1, d), w)


def _logf_kernel(x_ref, g_ref, w_ref, b_ref, o_ref):
    x = x_ref[...]
    y = x * lax.rsqrt(jnp.mean(x * x, axis=-1, keepdims=True) + RMS_EPS)
    h = (y * g_ref[...]).astype(BF16)
    f = _dot(h, w_ref[...]) + b_ref[...]
    o_ref[...] = _neg_softplus(-f)


def fox_logf(x, g, w_pad, b_pad, tm):
    m, d = x.shape
    n = w_pad.shape[1]
    return pl.pallas_call(
        _logf_kernel,
        grid=(m // tm,),
        in_specs=[
            pl.BlockSpec((tm, d), lambda i: (i, 0)),
            pl.BlockSpec((1, d), lambda i: (0, 0)),
            pl.BlockSpec((d, n), lambda i: (0, 0)),
            pl.BlockSpec((1, n), lambda i: (0, 0)),
        ],
        out_specs=pl.BlockSpec((tm, n), lambda i: (i, 0)),
        out_shape=jax.ShapeDtypeStruct((m, n), F32),
        compiler_params=_cparams(("parallel",), 32),
        name="fox_logf",
    )(x, g.reshape(1, d), w_pad, b_pad)


def _cumsum_kernel(x_ref, o_ref, *, blk):
    n = x_ref.shape[0]
    r = lax.broadcasted_iota(jnp.int32, (blk, blk), 0)
    c = lax.broadcasted_iota(jnp.int32, (blk, blk), 1)
    low = (c <= r).astype(BF16)

    def body(i, car):
        off = pl.multiple_of(i * blk, blk)
        hi, mid, lo = _split3(x_ref[pl.ds(off, blk), :])
        cum = _dot(low, hi) + _dot(low, mid) + _dot(low, lo) + car
        o_ref[pl.ds(off, blk), :] = cum
        return cum[blk - 1:blk, :]

    lax.fori_loop(0, n // blk, body, jnp.zeros((1, x_ref.shape[1]), F32))


def seq_cumsum(x):
    b, n, w = x.shape
    return pl.pallas_call(
        functools.partial(_cumsum_kernel, blk=128),
        grid=(b,),
        in_specs=[pl.BlockSpec((None, n, w), lambda i: (i, 0, 0))],
        out_specs=pl.BlockSpec((None, n, w), lambda i: (i, 0, 0)),
        out_shape=jax.ShapeDtypeStruct((b, n, w), F32),
        compiler_params=_cparams(("parallel",), 32),
        name="seq_cumsum",
    )(x)


def _sb_prompt_kernel(q_ref, k_ref, v_ref, o_ref, qb_scr, car_scr, *, tq, tk, hb):
    i = pl.program_id(2)
    qb_scr[...] = (q_ref[...] * (HEAD_DIM ** -0.5)).astype(BF16)
    o_ref[...] = jnp.zeros_like(o_ref)
    car_scr[...] = jnp.zeros_like(car_scr)
    t_pos = i * tq + lax.broadcasted_iota(jnp.int32, (tq, tk), 0)
    s_loc = lax.broadcasted_iota(jnp.int32, (tq, tk), 1)
    later = jnp.concatenate([_tri(tk), jnp.ones((tk, tk), BF16)], axis=1)
    nblk = (i + 1) * (tq // tk)

    def make_body(on_diagonal):
        def body(n, carry):
            off = pl.multiple_of((nblk - 1 - n) * tk, tk)
            before = (off + s_loc) < t_pos
            heads = [slice(h * HEAD_DIM, (h + 1) * HEAD_DIM) for h in range(hb)]
            zs = [_dot_nt(qb_scr[:, hc], k_ref[pl.ds(off, tk), hc].astype(BF16)) for hc in heads]
            lfs = [_neg_softplus(z) for z in zs]
            if on_diagonal:
                lfs = [jnp.where(before, lf, 0.0) for lf in lfs]
            sums = []
            for lf in lfs:
                hi, lo = _split2(lf)
                sums.append(_dot(hi, later) + _dot(lo, later))
            for h, hc in enumerate(heads):
                w = jnp.exp(zs[h] + lfs[h] + (sums[h][:, :tk] + car_scr[h]))
                if on_diagonal:
                    w = jnp.where(before, w, 0.0)
                o_ref[:, hc] += _dot(w.astype(BF16), v_ref[pl.ds(off, tk), hc].astype(BF16))
                car_scr[h] += sums[h][:, tk:]
            return carry
        return body

    lax.fori_loop(0, tq // tk, make_body(True), 0)
    lax.fori_loop(tq // tk, nblk, make_body(False), 0)


def sb_attn_prompt(proj, nb, seq, nh, tq=256, tk=256, hb=8):
    nq = seq // tq
    ng = nh // hb
    wb = hb * HEAD_DIM
    return pl.pallas_call(
        functools.partial(_sb_prompt_kernel, tq=tq, tk=tk, hb=hb),
        grid=(nb, ng, nq),
        in_specs=[
            pl.BlockSpec((tq, wb), lambda b, g, i: (b * nq + i, g)),
            pl.BlockSpec((seq, wb), lambda b, g, i: (b, ng + g)),
            pl.BlockSpec((seq, wb), lambda b, g, i: (b, 2 * ng + g)),
        ],
        out_specs=pl.BlockSpec((tq, wb), lambda b, g, i: (b * nq + i, g)),
        out_shape=jax.ShapeDtypeStruct((nb * seq, nh * HEAD_DIM), F32),
        scratch_shapes=[pltpu.VMEM((tq, wb), BF16), pltpu.VMEM((hb, tq, tk), F32)],
        compiler_params=_cparams(("parallel", "parallel", "arbitrary"), 52),
        name="sb_attn_prompt",
    )(proj, proj, proj)


def _fox_prompt_kernel(q_ref, k_ref, v_ref, cum_ref, crow_ref, o_ref, qb_scr, ct_scr, m_scr, l_scr,
                       *, tq, tk, hb):
    i = pl.program_id(2)
    reps = tk // HEAD_DIM
    qb_scr[...] = (q_ref[...] * (HEAD_DIM ** -0.5)).astype(BF16)
    o_ref[...] = jnp.zeros_like(o_ref)
    m_scr[...] = jnp.full(m_scr.shape, NEG_BIG, F32)
    l_scr[...] = jnp.zeros_like(l_scr)
    cum = cum_ref[...]
    lane = lax.broadcasted_iota(jnp.int32, cum.shape, 1)
    for h in range(hb):
        own = lane == pl.program_id(1) * hb + h
        ct_scr[h] = jnp.broadcast_to(jnp.sum(jnp.where(own, cum, 0.0), axis=1, keepdims=True),
                                     (tq, HEAD_DIM))
    t_pos = i * tq + lax.broadcasted_iota(jnp.int32, (tq, tk), 0)
    s_loc = lax.broadcasted_iota(jnp.int32, (tq, tk), 1)
    ones = jnp.ones((tk, HEAD_DIM), BF16)
    nblk = (i + 1) * (tq // tk)

    def wide(x):
        return jnp.concatenate([x] * reps, axis=1)

    def make_body(on_diagonal):
        def body(j, carry):
            off = pl.multiple_of(j * tk, tk)
            heads = [slice(h * HEAD_DIM, (h + 1) * HEAD_DIM) for h in range(hb)]
            zs = [_dot_nt(qb_scr[:, hc], k_ref[pl.ds(off, tk), hc].astype(BF16)) for hc in heads]
            ps = []
            alphas = []
            for h in range(hb):
                cs = crow_ref[h, :, pl.ds(off, tk)]
                s = zs[h] + (wide(ct_scr[h]) - cs)
                if on_diagonal:
                    s = jnp.where((off + s_loc) <= t_pos, s, NEG_BIG)
                m = m_scr[h]
                m_new = jnp.maximum(m, jnp.max(s, axis=1, keepdims=True))
                alphas.append(jnp.exp(m - m_new))
                ps.append(jnp.exp(s - wide(m_new)).astype(BF16))
                m_scr[h] = m_new
            for h, hc in enumerate(heads):
                v1 = jnp.concatenate([v_ref[pl.ds(off, tk), hc].astype(BF16), ones], axis=1)
                pv = _dot(ps[h], v1)
                l_scr[h] = alphas[h] * l_scr[h] + pv[:, HEAD_DIM:]
                o_ref[:, hc] = alphas[h] * o_ref[:, hc] + pv[:, :HEAD_DIM]
            return carry
        return body

    lax.fori_loop(0, nblk - tq // tk, make_body(False), 0)
    lax.fori_loop(nblk - tq // tk, nblk, make_body(True), 0)
    for h in range(hb):
        cols = slice(h * HEAD_DIM, (h + 1) * HEAD_DIM)
        o_ref[:, cols] = o_ref[:, cols] / l_scr[h]


def fox_attn_prompt(proj, cum, cum_row, nb, seq, nh, tq=256, tk=256, hb=8):
    nq = seq // tq
    ng = nh // hb
    wb = hb * HEAD_DIM
    return pl.pallas_call(
        functools.partial(_fox_prompt_kernel, tq=tq, tk=tk, hb=hb),
        grid=(nb, ng, nq),
        in_specs=[
            pl.BlockSpec((tq, wb), lambda b, g, i: (b * nq + i, g)),
            pl.BlockSpec((seq, wb), lambda b, g, i: (b, ng + g)),
            pl.BlockSpec((seq, wb), lambda b, g, i: (b, 2 * ng + g)),
            pl.BlockSpec((None, tq, PAGE), lambda b, g, i: (b, i, 0)),
            pl.BlockSpec((None, hb, 1, seq), lambda b, g, i: (b, g, 0, 0)),
        ],
        out_specs=pl.BlockSpec((tq, wb), lambda b, g, i: (b * nq + i, g)),
        out_shape=jax.ShapeDtypeStruct((nb * seq, nh * HEAD_DIM), F32),
        scratch_shapes=[pltpu.VMEM((tq, wb), BF16), pltpu.VMEM((hb, tq, HEAD_DIM), F32),
                        pltpu.VMEM((hb, tq, HEAD_DIM), F32), pltpu.VMEM((hb, tq, HEAD_DIM), F32)],
        compiler_params=_cparams(("parallel", "parallel", "arbitrary"), 52),
        name="fox_attn_prompt",
    )(proj, proj, proj, cum, cum_row)


GROUP_LANES = HEADS_PER_BLOCK * HEAD_DIM


def _head_select():
    lane_head = lax.broadcasted_iota(jnp.int32, (HEADS_PER_BLOCK, GROUP_LANES), 1) // HEAD_DIM
    return lane_head == lax.broadcasted_iota(jnp.int32, (HEADS_PER_BLOCK, GROUP_LANES), 0)


def _block_diag_queries(q8, nh, nq):
    sel = _head_select()
    rows = []
    for t in range(nq):
        for g in range(nh // HEADS_PER_BLOCK):
            qg = q8[t:t + 1, g * GROUP_LANES:(g + 1) * GROUP_LANES]
            rows.append(jnp.where(sel, jnp.broadcast_to(qg, (HEADS_PER_BLOCK, GROUP_LANES)), 0.0))
    rows.append(jnp.zeros((PAGE - nq * nh, GROUP_LANES), F32))
    return jnp.concatenate(rows, axis=0)


def _gather_heads(acc, nh, nq):
    sel = _head_select()
    rows = []
    for t in range(nq):
        groups = []
        for g in range(nh // HEADS_PER_BLOCK):
            r0 = t * nh + g * HEADS_PER_BLOCK
            groups.append(jnp.sum(jnp.where(sel, acc[r0:r0 + HEADS_PER_BLOCK, :], 0.0),
                                  axis=0, keepdims=True))
        rows.append(jnp.concatenate(groups, axis=1))
    rows.append(jnp.zeros((8 - nq, nh * HEAD_DIM), F32))
    return jnp.concatenate(rows, axis=0)


def _new_rows_page(rows8, nh):
    parts = [rows8[:, g * GROUP_LANES:(g + 1) * GROUP_LANES] for g in range(nh // HEADS_PER_BLOCK)]
    parts.append(jnp.zeros((PAGE - 8 * len(parts), GROUP_LANES), F32))
    return jnp.concatenate(parts, axis=0).astype(BF16)


def _load_page(ref, nh):
    n_rows = PAGE * nh // HEADS_PER_BLOCK
    parts = [ref[pl.ds(j, n_rows, stride=HEADS_PER_BLOCK), :] for j in range(HEADS_PER_BLOCK)]
    return jnp.concatenate(parts, axis=1).astype(BF16)


def _suffix_sum(x):
    n = x.shape[0]
    row = lax.broadcasted_iota(jnp.int32, x.shape, 0)
    d = 1
    while d < n:
        x = x + jnp.where(row < n - d, pltpu.roll(x, n - d, axis=0), 0.0)
        d *= 2
    return x


def _sb_decode_kernel(pt_ref, q_ref, kn_ref, vn_ref, *refs, nh, nq, pps):
    k_refs = refs[:pps]
    v_refs = refs[pps:2 * pps]
    o_ref = refs[2 * pps]
    qbd_scr, car_scr, acc_scr = refs[2 * pps + 1:]
    s = pl.program_id(1)
    nl = nq * nh

    @pl.when(s == 0)
    def _():
        qbd = _block_diag_queries(q_ref[...] * (HEAD_DIM ** -0.5), nh, nq).T
        qbd_scr[...] = qbd.astype(BF16)
        key = lax.broadcasted_iota(jnp.int32, (PAGE, PAGE), 0)
        vis = key < lax.broadcasted_iota(jnp.int32, (PAGE, PAGE), 1) // nh
        z = _dot(_new_rows_page(kn_ref[...], nh), qbd_scr[...])
        suf = _suffix_sum(jnp.where(vis, _neg_softplus(z), 0.0))
        w = jnp.where(vis, jnp.exp(z + suf), 0.0)
        car_scr[...] = suf[0:1, :]
        acc_scr[...] = _dot(w.T[:nl, :].astype(BF16), _new_rows_page(vn_ref[...], nh))

    def scores(r):
        return _dot(_load_page(k_refs[r], nh), qbd_scr[...])

    ahead = 2
    zs = [scores(r) for r in range(min(ahead, pps))]
    car = car_scr[...]
    acc = acc_scr[...]
    for r in range(pps):
        suf = _suffix_sum(_neg_softplus(zs[r]))
        w = jnp.exp(zs[r] + suf + car)
        car = car + suf[0:1, :]
        if r + ahead < pps:
            zs.append(scores(r + ahead))
        acc = acc + _dot(w.T[:nl, :].astype(BF16), _load_page(v_refs[r], nh))
    car_scr[...] = car
    acc_scr[...] = acc

    @pl.when(s == pl.num_programs(1) - 1)
    def _():
        o_ref[...] = _gather_heads(acc_scr[...], nh, nq)


def _page_specs(rows, n_pages, pps):
    specs = []
    for r in range(pps):
        def imap(b, s, pt, r=r):
            return (pt[b, n_pages - 1 - (s * pps + r)], 0, 0)
        specs.append(pl.BlockSpec((None, rows, HEAD_DIM), imap))
    return specs


def sb_attn_decode(page_table, q8, kn8, vn8, pool_k, pool_v, nh, nq, pps):
    assert nh == HEADS_PER_BLOCK
    nb, n_pages = page_table.shape
    w = nh * HEAD_DIM
    row_spec = pl.BlockSpec((None, 8, w), lambda b, s, pt: (b, 0, 0))
    grid_spec = pltpu.PrefetchScalarGridSpec(
        num_scalar_prefetch=1,
        grid=(nb, n_pages // pps),
        in_specs=[row_spec, row_spec, row_spec]
        + _page_specs(PAGE * nh, n_pages, pps) + _page_specs(PAGE * nh, n_pages, pps),
        out_specs=row_spec,
        scratch_shapes=[
            pltpu.VMEM((GROUP_LANES, PAGE), BF16),
            pltpu.VMEM((1, PAGE), F32),
            pltpu.VMEM((nq * nh, GROUP_LANES), F32),
        ],
    )
    return pl.pallas_call(
        functools.partial(_sb_decode_kernel, nh=nh, nq=nq, pps=pps),
        grid_spec=grid_spec,
        out_shape=jax.ShapeDtypeStruct((nb, 8, w), F32),
        compiler_params=_cparams(("arbitrary", "arbitrary"), 56),
        name="sb_attn_decode",
    )(page_table, q8, kn8, vn8, *([pool_k] * pps), *([pool_v] * pps))


def _tile_lanes(x16, nh, nq):
    out = x16
    for t in range(1, nq):
        out = out + pltpu.roll(x16, t * nh, axis=1)
    return out


def _fox_decode_kernel(pt_ref, q_ref, kn_ref, vn_ref, lfn_ref, *refs, nh, nq, pps):
    ng = nh // HEADS_PER_BLOCK
    k_refs = refs[:pps]
    v_refs = refs[pps:2 * pps]
    lf_refs = refs[2 * pps:3 * pps]
    o_ref = refs[3 * pps]
    qbd_scr, lfpad_scr, bias_scr, car_scr, pre_scr, m_scr, l_scr, acc_scr = refs[3 * pps + 1:]
    s = pl.program_id(1)
    nl = nq * nh
    n_rows = PAGE * ng
    row_group = lax.broadcasted_iota(jnp.int32, (n_rows, PAGE), 0) % ng
    lane_group = (lax.broadcasted_iota(jnp.int32, (n_rows, PAGE), 1) % nh) // HEADS_PER_BLOCK
    valid = row_group == lane_group

    def to_col(row):
        return jnp.broadcast_to(row, (PAGE, PAGE)).T[:nl, 0:1]

    def to_rows(p):
        blocks = [p[i * PAGE:(i + 1) * PAGE, :].T for i in range(p.shape[0] // PAGE)]
        return jnp.concatenate(blocks, axis=1)[:nl, :].astype(BF16)

    @pl.when(s == 0)
    def _():
        qbd = _block_diag_queries(q_ref[...] * (HEAD_DIM ** -0.5), nh, nq).T
        qbd_scr[...] = qbd.astype(BF16)
        lfpad_scr[...] = jnp.zeros((PAGE, PAGE), F32)
        lfn = lfn_ref[...]
        key8 = lax.broadcasted_iota(jnp.int32, (8, PAGE), 0)
        qry8 = lax.broadcasted_iota(jnp.int32, (8, PAGE), 1) // nh
        pre_q = jnp.sum(jnp.where(key8 <= qry8, lfn, 0.0), axis=0, keepdims=True)
        run = jnp.zeros((1, PAGE), F32)
        pre_rows = []
        for j in range(8):
            run = run + lfn[j:j + 1, :]
            pre_rows.append(run)
        pre_k = jnp.concatenate(pre_rows, axis=0)
        nr = 8 * ng
        rown = lax.broadcasted_iota(jnp.int32, (nr, PAGE), 0)
        lanen = lax.broadcasted_iota(jnp.int32, (nr, PAGE), 1)
        seen = ((rown // 8) == (lanen % nh) // HEADS_PER_BLOCK) & ((rown % 8) <= lanen // nh)
        z = _dot(_new_rows_page(kn_ref[...], nh), qbd_scr[...])[:nr, :]
        sc = jnp.where(seen, z + jnp.concatenate([pre_q - pre_k] * ng, axis=0), NEG_BIG)
        m = jnp.max(sc, axis=0, keepdims=True)
        p = jnp.exp(sc - m)
        pt = to_rows(jnp.concatenate([p, jnp.zeros((PAGE - nr, PAGE), F32)], axis=0))
        acc_scr[...] = _dot(pt, _new_rows_page(vn_ref[...], nh))
        l_scr[...] = jnp.sum(p, axis=0, keepdims=True)
        m_scr[...] = m
        pre_scr[...] = pre_q
        car_scr[...] = jnp.zeros((1, PAGE), F32)

    zs = [_dot(_load_page(k_refs[r], nh), qbd_scr[...]) for r in range(pps)]
    car = car_scr[...]
    pre_q = pre_scr[...]
    scores = []
    for r in range(pps):
        lfpad_scr[:, 0:nh] = lf_refs[r][...]
        lf = _tile_lanes(lfpad_scr[...], nh, nq)
        suf = _suffix_sum(lf)
        bias = (suf - lf) + car + pre_q
        for g in range(ng):
            bias_scr[pl.ds(g, PAGE, stride=ng), :] = bias
        scores.append(jnp.where(valid, zs[r] + bias_scr[...], NEG_BIG))
        car = car + suf[0:1, :]
    car_scr[...] = car

    m_old = m_scr[...]
    m_new = m_old
    for sc in scores:
        m_new = jnp.maximum(m_new, jnp.max(sc, axis=0, keepdims=True))
    alpha = jnp.exp(m_old - m_new)
    l = alpha * l_scr[...]
    acc = acc_scr[...] * to_col(alpha)
    for r in range(pps):
        p = jnp.exp(scores[r] - m_new)
        l = l + jnp.sum(p, axis=0, keepdims=True)
        acc = acc + _dot(to_rows(p), _load_page(v_refs[r], nh))
    acc_scr[...] = acc
    l_scr[...] = l
    m_scr[...] = m_new

    @pl.when(s == pl.num_programs(1) - 1)
    def _():
        o_ref[...] = _gather_heads(acc_scr[...] / to_col(l_scr[...]), nh, nq)


def fox_attn_decode(page_table, q8, kn8, vn8, lfn8, pool_k, pool_v, pool_lf, nh, nq, pps):
    nb, n_pages = page_table.shape
    w = nh * HEAD_DIM
    ng = nh // HEADS_PER_BLOCK
    row_spec = pl.BlockSpec((None, 8, w), lambda b, s, pt: (b, 0, 0))
    lf_specs = []
    for r in range(pps):
        def imap(b, s, pt, r=r):
            return (pt[b, n_pages - 1 - (s * pps + r)], 0, 0)
        lf_specs.append(pl.BlockSpec((None, PAGE, nh), imap))
    grid_spec = pltpu.PrefetchScalarGridSpec(
        num_scalar_prefetch=1,
        grid=(nb, n_pages // pps),
        in_specs=[row_spec, row_spec, row_spec,
                  pl.BlockSpec((None, 8, PAGE), lambda b, s, pt: (b, 0, 0))]
        + _page_specs(PAGE * nh, n_pages, pps) + _page_specs(PAGE * nh, n_pages, pps) + lf_specs,
        out_specs=row_spec,
        scratch_shapes=[
            pltpu.VMEM((GROUP_LANES, PAGE), BF16),
            pltpu.VMEM((PAGE, PAGE), F32),
            pltpu.VMEM((PAGE * ng, PAGE), F32),
            pltpu.VMEM((1, PAGE), F32),
            pltpu.VMEM((1, PAGE), F32),
            pltpu.VMEM((1, PAGE), F32),
            pltpu.VMEM((1, PAGE), F32),
            pltpu.VMEM((nq * nh, GROUP_LANES), F32),
        ],
    )
    return pl.pallas_call(
        functools.partial(_fox_decode_kernel, nh=nh, nq=nq, pps=pps),
        grid_spec=grid_spec,
        out_shape=jax.ShapeDtypeStruct((nb, 8, w), F32),
        compiler_params=_cparams(("arbitrary", "arbitrary"), 56),
        name="fox_attn_decode",
    )(page_table, q8, kn8, vn8, lfn8, *([pool_k] * pps), *([pool_v] * pps), *([pool_lf] * pps))


def _ssm_prep_kernel(are_ref, aim_ref, ldt_ref, bre_ref, bim_ref,
                     pwr_ref, pwi_ref, bbr_ref, bbi_ref):
    ar = are_ref[...]
    ai = aim_ref[...]
    dt = jnp.exp(ldt_ref[...])
    mag = jnp.exp(ar * dt)
    abr = mag * jnp.cos(ai * dt)
    abi = mag * jnp.sin(ai * dt)
    den = ar * ar + ai * ai
    nr = abr - 1.0
    ni = abi
    cr = (nr * ar + ni * ai) / den
    ci = (ni * ar - nr * ai) / den
    br = bre_ref[...]
    bi = bim_ref[...]
    bbr_ref[...] = cr * br - ci * bi
    bbi_ref[...] = cr * bi + ci * br
    pr, pi = abr, abi
    pwr_ref[0] = pr
    pwi_ref[0] = pi
    for k in range(1, SCAN_ROWS):
        pr, pi = pr * abr - pi * abi, pr * abi + pi * abr
        pwr_ref[k] = pr
        pwi_ref[k] = pi


def ssm_prep(a_re, a_im, log_dt, b_re_t, b_im_t):
    g, p = a_re.shape
    c = b_re_t.shape[1]
    return pl.pallas_call(
        _ssm_prep_kernel,
        out_shape=(jax.ShapeDtypeStruct((SCAN_ROWS, g, 1, p), F32),
                   jax.ShapeDtypeStruct((SCAN_ROWS, g, 1, p), F32),
                   jax.ShapeDtypeStruct((g, c, p), F32), jax.ShapeDtypeStruct((g, c, p), F32)),
        name="ssm_prep",
    )(a_re.reshape(g, 1, p), a_im.reshape(g, 1, p), log_dt.reshape(g, 1, 1), b_re_t, b_im_t)


def _ssd_prep_kernel(are_ref, aim_ref, ldt_ref, bre_ref, bim_ref, cre_ref, cim_ref,
                     hbr_ref, hbi_ref, cpr_ref, cpi_ref, awr_ref, awi_ref):
    ar = are_ref[...]
    ai = aim_ref[...]
    dt = jnp.exp(ldt_ref[...])
    mag = jnp.exp(ar * dt)
    abr = mag * jnp.cos(ai * dt)
    abi = mag * jnp.sin(ai * dt)
    den = ar * ar + ai * ai
    nr = abr - 1.0
    ni = abi
    cr = (nr * ar + ni * ai) / den
    ci = (ni * ar - nr * ai) / den
    br = bre_ref[...]
    bi = bim_ref[...]
    bbr = cr * br - ci * bi
    bbi = cr * bi + ci * br
    c_r = cre_ref[...]
    c_i = cim_ref[...]
    pr = jnp.ones_like(abr)
    pi = jnp.zeros_like(abr)
    for tau in range(CHUNK_T + 1):
        cpr_ref[tau] = c_r * pr - c_i * pi
        cpi_ref[tau] = c_r * pi + c_i * pr
        if tau < CHUNK_T:
            hbr_ref[tau] = pr * bbr - pi * bbi
            hbi_ref[tau] = pr * bbi + pi * bbr
            pr, pi = pr * abr - pi * abi, pr * abi + pi * abr
    for k in range(awr_ref.shape[0]):
        awr_ref[k] = pr
        awi_ref[k] = pi
        pr, pi = pr * pr - pi * pi, 2.0 * pr * pi


def ssd_prep(a_re, a_im, log_dt, b_re_t, b_im_t, c_re, c_im, n_dbl):
    g, p = a_re.shape
    c = b_re_t.shape[1]
    full = lambda n: jax.ShapeDtypeStruct((n, g, c, p), F32)
    pows = jax.ShapeDtypeStruct((n_dbl, g, 1, p), F32)
    return pl.pallas_call(
        _ssd_prep_kernel,
        out_shape=(full(CHUNK_T), full(CHUNK_T), full(CHUNK_T + 1), full(CHUNK_T + 1), pows, pows),
        name="ssd_prep",
    )(a_re.reshape(g, 1, p), a_im.reshape(g, 1, p), log_dt.reshape(g, 1, 1), b_re_t, b_im_t, c_re, c_im)


def _dot_nt_f32(a, b):
    a3 = _split3(a)
    b3 = _split3(b)
    acc = None
    for i in range(3):
        for j in range(3 - i):
            t = _dot_nt(a3[i], b3[j])
            acc = t if acc is None else acc + t
    return acc


def _ssd_taps_kernel(cpr_ref, cpi_ref, bbr_ref, bbi_ref, k_ref):
    _, c, p = cpr_ref.shape
    a_r = cpr_ref[0:CHUNK_T].reshape(CHUNK_T * c, p)
    a_i = cpi_ref[0:CHUNK_T].reshape(CHUNK_T * c, p)
    k_ref[...] = _dot_nt_f32(a_r, bbr_ref[...]) - _dot_nt_f32(a_i, bbi_ref[...])


def ssd_taps(cp_r, cp_i, hb_r, hb_i):
    t1, g, c, p = cp_r.shape
    cp_spec = pl.BlockSpec((t1, None, c, p), lambda i: (0, i, 0, 0))
    bb_spec = pl.BlockSpec((None, None, c, p), lambda i: (0, i, 0, 0))
    return pl.pallas_call(
        _ssd_taps_kernel,
        grid=(g,),
        in_specs=[cp_spec, cp_spec, bb_spec, bb_spec],
        out_specs=pl.BlockSpec((None, CHUNK_T * c, c), lambda i: (i, 0, 0)),
        out_shape=jax.ShapeDtypeStruct((g, CHUNK_T * c, c), F32),
        compiler_params=_cparams(("parallel",), 32),
        name="ssd_taps",
    )(cp_r, cp_i, hb_r, hb_i)


def _ssd_kernel(u_ref, my_ref, mh_ref, mc_ref, a1_ref, a2_ref, d_ref, y_ref, h_ref, *, nb, nk):
    u = u_ref[...]
    ub = u.astype(BF16)
    x = _dot(ub, mh_ref[...])
    half = x.shape[1] // 2
    row = lax.broadcasted_iota(jnp.int32, (nk, x.shape[1]), 0)
    before = []
    for b in range(nb):
        s = x[b * nk:(b + 1) * nk, :]
        d, k = 1, 0
        while d < nk:
            sh = jnp.where(row >= d, pltpu.roll(s, d, axis=0), 0.0)
            s = s + a1_ref[k:k + 1, :] * sh + a2_ref[k:k + 1, :] * pltpu.roll(sh, half, axis=1)
            d, k = 2 * d, k + 1
        h_ref[b:b + 1, :] = s[nk - 1:nk, :]
        before.append(jnp.where(row >= 1, pltpu.roll(s, 1, axis=0), 0.0))
    s_prev = jnp.concatenate(before, axis=0).astype(BF16)
    y_ref[...] = _dot(ub, my_ref[...]) + _dot(s_prev, mc_ref[...]) + u * d_ref[...]


def ssm_prompt(u_g, m_y, m_h, m_c, a1, a2, d_row, nb, nk):
    g, rows, tc = u_g.shape
    sl = m_h.shape[2]
    per_group = lambda r, c: pl.BlockSpec((None, r, c), lambda i: (i, 0, 0))
    return pl.pallas_call(
        functools.partial(_ssd_kernel, nb=nb, nk=nk),
        grid=(g,),
        in_specs=[per_group(rows, tc), per_group(tc, tc), per_group(tc, sl), per_group(sl, tc),
                  per_group(a1.shape[1], sl), per_group(a2.shape[1], sl), per_group(1, tc)],
        out_specs=(per_group(rows, tc), per_group(nb, sl)),
        out_shape=(jax.ShapeDtypeStruct((g, rows, tc), F32), jax.ShapeDtypeStruct((g, nb, sl), F32)),
        compiler_params=_cparams(("parallel",), 32),
        name="ssm_prompt",
    )(u_g, m_y, m_h, m_c, a1, a2, d_row)


def _ssd_matrices(taps, hb_r, hb_i, cp_r, cp_i, aw_r, aw_i, d):
    g = taps.shape[0]
    c = taps.shape[2]
    t = CHUNK_T
    k = taps.reshape(g, t, c, c)
    lag = jnp.arange(t)[None, :] - jnp.arange(t)[:, None]
    kk = jnp.where((lag >= 0)[None, :, :, None, None], k[:, jnp.clip(lag, 0, t - 1)], 0.0)
    m_y = jnp.transpose(kk, (0, 1, 4, 2, 3)).reshape(g, t * c, t * c)
    to_rows = lambda hb: jnp.transpose(hb[::-1], (1, 0, 2, 3)).reshape(g, t * c, -1)
    m_h = jnp.concatenate([to_rows(hb_r), to_rows(hb_i)], axis=2)
    to_cols = lambda cp: jnp.transpose(cp[1:], (1, 3, 0, 2)).reshape(g, -1, t * c)
    m_c = jnp.concatenate([to_cols(cp_r), -to_cols(cp_i)], axis=1)
    aw_r = jnp.transpose(aw_r[:, :, 0, :], (1, 0, 2))
    aw_i = jnp.transpose(aw_i[:, :, 0, :], (1, 0, 2))
    a1 = jnp.concatenate([aw_r, aw_r], axis=2)
    a2 = jnp.concatenate([-aw_i, aw_i], axis=2)
    d_row = jnp.tile(d.reshape(g, 1, c), (1, 1, t))
    return m_y.astype(BF16), m_h.astype(BF16), m_c.astype(BF16), a1, a2, d_row


def _ssm_sample_kernel(u_ref, h0r_ref, h0i_ref, wbr_ref, wbi_ref, wcr_ref, wci_ref, d_ref,
                       pwr_ref, pwi_ref, y_ref, hr_ref, hi_ref, *, steps):
    a_r = pwr_ref[0:1, :]
    a_i = pwi_ref[0:1, :]
    h_r = h0r_ref[...]
    h_i = h0i_ref[...]
    for t in range(steps):
        u = u_ref[t]
        ub = u.astype(BF16)
        h_r, h_i = (a_r * h_r - a_i * h_i + _dot(ub, wbr_ref[...]),
                    a_r * h_i + a_i * h_r + _dot(ub, wbi_ref[...]))
        y_ref[t] = (_dot(h_r.astype(BF16), wcr_ref[...]) - _dot(h_i.astype(BF16), wci_ref[...])
                    + d_ref[...] * u)
    hr_ref[...] = h_r
    hi_ref[...] = h_i


def ssm_sample(u_t, h0_r, h0_i, wbr, wbi, wcr, wci, d, pw_r, pw_i):
    steps, nb, _ = u_t.shape
    nchunk, cw, lanes = wbr.shape
    return pl.pallas_call(
        functools.partial(_ssm_sample_kernel, steps=steps),
        grid=(nchunk,),
        in_specs=[
            pl.BlockSpec((steps, nb, cw), lambda j: (0, 0, j)),
            pl.BlockSpec((nb, lanes), lambda j: (0, j)),
            pl.BlockSpec((nb, lanes), lambda j: (0, j)),
            pl.BlockSpec((None, cw, lanes), lambda j: (j, 0, 0)),
            pl.BlockSpec((None, cw, lanes), lambda j: (j, 0, 0)),
            pl.BlockSpec((None, lanes, cw), lambda j: (j, 0, 0)),
            pl.BlockSpec((None, lanes, cw), lambda j: (j, 0, 0)),
            pl.BlockSpec((1, cw), lambda j: (0, j)),
            pl.BlockSpec((SCAN_ROWS, lanes), lambda j: (0, j)),
            pl.BlockSpec((SCAN_ROWS, lanes), lambda j: (0, j)),
        ],
        out_specs=(
            pl.BlockSpec((steps, nb, cw), lambda j: (0, 0, j)),
            pl.BlockSpec((nb, lanes), lambda j: (0, j)),
            pl.BlockSpec((nb, lanes), lambda j: (0, j)),
        ),
        out_shape=(jax.ShapeDtypeStruct((steps, nb, nchunk * cw), F32),
                   jax.ShapeDtypeStruct((nb, nchunk * lanes), F32),
                   jax.ShapeDtypeStruct((nb, nchunk * lanes), F32)),
        compiler_params=_cparams(("parallel",), 32),
        name="ssm_sample",
    )(u_t, h0_r, h0_i, wbr, wbi, wcr, wci, d, pw_r, pw_i)


def _even_out_kernel(oa_ref, ga_ref, yb_ref, gb_ref, x_ref, wg_ref, bg_ref, wo_ref, o_ref):
    half = oa_ref.shape[1]
    ya = oa_ref[...] * jax.nn.silu(ga_ref[...])
    yb = jax.nn.gelu(yb_ref[...])
    yb = yb * jax.nn.sigmoid(_dot(yb.astype(BF16), wg_ref[...]) + bg_ref[...])
    yb = yb * jax.nn.silu(gb_ref[...])
    o_ref[...] = (x_ref[...] + _dot(ya.astype(BF16), wo_ref[0:half, :])
                  + _dot(yb.astype(BF16), wo_ref[half:, :]))


def even_out(o_a, y_b, proj, x, w_glu, b_glu, w_out, tm=256):
    m, half = o_a.shape
    tm = min(tm, m)
    d = x.shape[1]
    return pl.pallas_call(
        _even_out_kernel,
        grid=(m // tm,),
        in_specs=[
            pl.BlockSpec((tm, half), lambda i: (i, 0)),
            pl.BlockSpec((tm, half), lambda i: (i, 3)),
            pl.BlockSpec((tm, half), lambda i: (i, 0)),
            pl.BlockSpec((tm, half), lambda i: (i, 5)),
            pl.BlockSpec((tm, d), lambda i: (i, 0)),
            pl.BlockSpec((half, half), lambda i: (0, 0)),
            pl.BlockSpec((1, half), lambda i: (0, 0)),
            pl.BlockSpec((2 * half, d), lambda i: (0, 0)),
        ],
        out_specs=pl.BlockSpec((tm, d), lambda i: (i, 0)),
        out_shape=jax.ShapeDtypeStruct((m, d), F32),
        compiler_params=_cparams(("parallel",), 48),
        name="even_out",
    )(o_a, proj, y_b, proj, x, w_glu, b_glu, w_out)


def _odd_out_kernel(o_ref_in, g_ref, x_ref, wo_ref, gf_ref, y_ref):
    y = o_ref_in[...] * jax.nn.silu(g_ref[...])
    r = x_ref[...] + _dot(y.astype(BF16), wo_ref[...])
    n = r * lax.rsqrt(jnp.mean(r * r, axis=-1, keepdims=True) + RMS_EPS)
    y_ref[...] = n * gf_ref[...]


def odd_out(o, proj, x, w_out, g_final, tm=256):
    m, d = x.shape
    tm = min(tm, m)
    w = o.shape[1]
    return pl.pallas_call(
        _odd_out_kernel,
        grid=(m // tm,),
        in_specs=[
            pl.BlockSpec((tm, w), lambda i: (i, 0)),
            pl.BlockSpec((tm, w), lambda i: (i, 3)),
            pl.BlockSpec((tm, d), lambda i: (i, 0)),
            pl.BlockSpec((w, d), lambda i: (0, 0)),
            pl.BlockSpec((1, d), lambda i: (0, 0)),
        ],
        out_specs=pl.BlockSpec((tm, d), lambda i: (i, 0)),
        out_shape=jax.ShapeDtypeStruct((m, d), F32),
        compiler_params=_cparams(("parallel",), 48),
        name="odd_out",
    )(o, proj, x, w_out, g_final.reshape(1, d))


def _pad_rows8(x):
    return jnp.pad(x, ((0, 0), (0, 8 - x.shape[1]), (0, 0)))


def _block_diag(x):
    nc, pc, a, b = x.shape
    eye = jnp.eye(pc, dtype=x.dtype)
    return jnp.einsum("jgab,gh->jgahb", x, eye).reshape(nc, pc * a, pc * b)


def kernel(x_prompt, x_sample, cache_sb_k, cache_sb_v, state_ssm_re, state_ssm_im, cache_fox_k, cache_fox_v, cache_fox_logf, page_table, norm_g, w_in_even, ssm_a_re, ssm_a_im, ssm_log_dt, ssm_b_re, ssm_b_im, ssm_c_re, ssm_c_im, ssm_d, w_glu, b_glu, w_out_even, w_in_odd, b_f, w_out_odd, norm_f_g):
    nbp, seq, d_model = x_prompt.shape
    nbs, dec = x_sample.shape[:2]
    mp = nbp * seq
    ms = nbs * dec
    w_sb = d_model // 2
    h_sb = w_sb // HEAD_DIM
    w_ssm = d_model - w_sb
    g_ssm = w_ssm // SSM_GROUP
    h_fox = d_model // HEAD_DIM
    n_pool = cache_sb_k.shape[1]
    groups_per_chunk = HEAD_DIM // SSM_GROUP
    n_chunks = g_ssm // groups_per_chunk

    xp = x_prompt.reshape(mp, d_model)
    xs = x_sample.reshape(ms, d_model)

    w_in0 = w_in_even[0].astype(BF16)
    proj0_p = norm_proj(xp, norm_g[0], w_in0, tm=PROJ_ROWS, tn=PROJ_COLS)
    proj0_s = norm_proj(xs, norm_g[0], w_in0, tm=ms, tn=PROJ_COLS)
    k0_s = proj0_s[:, w_sb:2 * w_sb]
    v0_s = proj0_s[:, 2 * w_sb:3 * w_sb]

    oa_p = sb_attn_prompt(proj0_p, nbp, seq, h_sb)
    q8 = _pad_rows8(proj0_s[:, :w_sb].reshape(nbs, dec, w_sb))
    kn8 = _pad_rows8(k0_s.reshape(nbs, dec, w_sb))
    vn8 = _pad_rows8(v0_s.reshape(nbs, dec, w_sb))
    oa_s = sb_attn_decode(page_table, q8, kn8, vn8,
                          cache_sb_k.reshape(n_pool, PAGE * h_sb, HEAD_DIM),
                          cache_sb_v.reshape(n_pool, PAGE * h_sb, HEAD_DIM),
                          h_sb, dec, SB_PAGES_PER_STEP)[:, :dec].reshape(ms, w_sb)

    b_re_t = jnp.swapaxes(ssm_b_re[0], 1, 2)
    b_im_t = jnp.swapaxes(ssm_b_im[0], 1, 2)
    pw_r, pw_i, bb_r, bb_i = ssm_prep(ssm_a_re[0], ssm_a_im[0], ssm_log_dt[0], b_re_t, b_im_t)
    n_state = g_ssm * P_SSM
    pw_r = pw_r.reshape(SCAN_ROWS, n_state)
    pw_i = pw_i.reshape(SCAN_ROWS, n_state)
    wbr = _block_diag(bb_r.reshape(n_chunks, groups_per_chunk, SSM_GROUP, P_SSM)).astype(BF16)
    wbi = _block_diag(bb_i.reshape(n_chunks, groups_per_chunk, SSM_GROUP, P_SSM)).astype(BF16)
    c_re_t = jnp.swapaxes(ssm_c_re[0], 1, 2)
    c_im_t = jnp.swapaxes(ssm_c_im[0], 1, 2)
    wcr = _block_diag(c_re_t.reshape(n_chunks, groups_per_chunk, P_SSM, SSM_GROUP)).astype(BF16)
    wci = _block_diag(c_im_t.reshape(n_chunks, groups_per_chunk, P_SSM, SSM_GROUP)).astype(BF16)
    d_row = ssm_d[0].reshape(1, w_ssm)

    nk = seq // CHUNK_T
    n_dbl = max(1, (nk - 1).bit_length())
    hb_r, hb_i, cp_r, cp_i, aw_r, aw_i = ssd_prep(ssm_a_re[0], ssm_a_im[0], ssm_log_dt[0], b_re_t, b_im_t,
                                                  ssm_c_re[0], ssm_c_im[0], n_dbl)
    taps = ssd_taps(cp_r, cp_i, hb_r, hb_i)
    m_y, m_h, m_c, a1, a2, d_grp = _ssd_matrices(taps, hb_r, hb_i, cp_r, cp_i, aw_r, aw_i, ssm_d[0])
    u_g = proj0_p[:, 4 * w_sb:4 * w_sb + w_ssm].reshape(nbp, nk, CHUNK_T, g_ssm, SSM_GROUP)
    u_g = jnp.transpose(u_g, (3, 0, 1, 2, 4)).reshape(g_ssm, nbp * nk, CHUNK_T * SSM_GROUP)
    y_g, h_g = ssm_prompt(u_g, m_y, m_h, m_c, a1, a2, d_grp, nbp, nk)
    yb_p = jnp.transpose(y_g.reshape(g_ssm, nbp, nk, CHUNK_T, SSM_GROUP), (1, 2, 3, 0, 4)).reshape(mp, w_ssm)
    hr_p = jnp.transpose(h_g[:, :, :P_SSM], (1, 0, 2))
    hi_p = jnp.transpose(h_g[:, :, P_SSM:], (1, 0, 2))
    u_s = jnp.swapaxes(proj0_s[:, 4 * w_sb:4 * w_sb + w_ssm].reshape(nbs, dec, w_ssm), 0, 1)
    yb_s, hr_s, hi_s = ssm_sample(u_s, state_ssm_re[0].reshape(nbs, n_state),
                                  state_ssm_im[0].reshape(nbs, n_state),
                                  wbr, wbi, wcr, wci, d_row, pw_r, pw_i)
    yb_s = jnp.swapaxes(yb_s, 0, 1).reshape(ms, w_ssm)

    w_glu0 = w_glu[0].astype(BF16)
    b_glu0 = b_glu[0].reshape(1, w_ssm)
    w_out0 = w_out_even[0].astype(BF16)
    x1_p = even_out(oa_p, yb_p, proj0_p, xp, w_glu0, b_glu0, w_out0)
    x1_s = even_out(oa_s, yb_s, proj0_s, xs, w_glu0, b_glu0, w_out0)

    w_main = w_in_odd[0].astype(BF16)
    w_f = jnp.pad(w_main[:, 4 * d_model:], ((0, 0), (0, PAGE - h_fox)))
    b_pad = jnp.pad(b_f[0], (0, PAGE - h_fox)).reshape(1, PAGE)
    proj1_p = norm_proj(x1_p, norm_g[1], w_main, tm=PROJ_ROWS, tn=PROJ_COLS)
    proj1_s = norm_proj(x1_s, norm_g[1], w_main, tm=ms, tn=PROJ_COLS)
    logf_p = fox_logf(x1_p, norm_g[1], w_f, b_pad, tm=512).reshape(nbp, seq, PAGE)
    logf_s = fox_logf(x1_s, norm_g[1], w_f, b_pad, tm=ms)[:, :h_fox].reshape(nbs, dec, h_fox)

    cum = seq_cumsum(logf_p)
    cum_row = jnp.transpose(cum[:, :, :h_fox], (0, 2, 1))[:, :, None, :]
    o_p = fox_attn_prompt(proj1_p, cum, cum_row, nbp, seq, h_fox)

    k1_s = proj1_s[:, d_model:2 * d_model]
    v1_s = proj1_s[:, 2 * d_model:3 * d_model]
    q8 = _pad_rows8(proj1_s[:, :d_model].reshape(nbs, dec, d_model))
    kn8 = _pad_rows8(k1_s.reshape(nbs, dec, d_model))
    vn8 = _pad_rows8(v1_s.reshape(nbs, dec, d_model))
    lfn8 = _pad_rows8(jnp.pad(jnp.tile(logf_s, (1, 1, dec)), ((0, 0), (0, 0), (0, PAGE - dec * h_fox))))
    o_s = fox_attn_decode(page_table, q8, kn8, vn8, lfn8,
                          cache_fox_k.reshape(n_pool, PAGE * h_fox, HEAD_DIM),
                          cache_fox_v.reshape(n_pool, PAGE * h_fox, HEAD_DIM),
                          cache_fox_logf.reshape(n_pool, PAGE, h_fox),
                          h_fox, dec, FOX_PAGES_PER_STEP)[:, :dec].reshape(ms, d_model)

    w_out1 = w_out_odd[0].astype(BF16)
    y_prompt = odd_out(o_p, proj1_p, x1_p, w_out1, norm_f_g).reshape(nbp, seq, d_model)
    y_sample = odd_out(o_s, proj1_s, x1_s, w_out1, norm_f_g).reshape(nbs, dec, d_model)

    sb_k_prompt = proj0_p[:, w_sb:2 * w_sb].reshape(1, nbp, seq, h_sb, HEAD_DIM)
    sb_v_prompt = proj0_p[:, 2 * w_sb:3 * w_sb].reshape(1, nbp, seq, h_sb, HEAD_DIM)
    sb_k_sample = k0_s.reshape(1, nbs, dec, h_sb, HEAD_DIM)
    sb_v_sample = v0_s.reshape(1, nbs, dec, h_sb, HEAD_DIM)
    ssm_re_prompt = hr_p.reshape(1, nbp, g_ssm, P_SSM)
    ssm_im_prompt = hi_p.reshape(1, nbp, g_ssm, P_SSM)
    ssm_re_sample = hr_s.reshape(1, nbs, g_ssm, P_SSM)
    ssm_im_sample = hi_s.reshape(1, nbs, g_ssm, P_SSM)
    fox_k_prompt = proj1_p[:, d_model:2 * d_model].reshape(1, nbp, seq, h_fox, HEAD_DIM)
    fox_v_prompt = proj1_p[:, 2 * d_model:3 * d_model].reshape(1, nbp, seq, h_fox, HEAD_DIM)
    fox_logf_prompt = logf_p[:, :, :h_fox].reshape(1, nbp, seq, h_fox)
    fox_k_sample = k1_s.reshape(1, nbs, dec, h_fox, HEAD_DIM)
    fox_v_sample = v1_s.reshape(1, nbs, dec, h_fox, HEAD_DIM)
    fox_logf_sample = logf_s.reshape(1, nbs, dec, h_fox)
    return (y_prompt, y_sample, sb_k_prompt, sb_v_prompt, sb_k_sample, sb_v_sample,
            ssm_re_prompt, ssm_im_prompt, ssm_re_sample, ssm_im_sample,
            fox_k_prompt, fox_v_prompt, fox_logf_prompt, fox_k_sample, fox_v_sample, fox_logf_sample)
```

```python
import functools

import jax
import jax.numpy as jnp
from jax import lax
from jax.experimental import pallas as pl
from jax.experimental.pallas import tpu as pltpu

F32 = jnp.float32
BF16 = jnp.bfloat16

HEAD_DIM = 128
SSM_GROUP = 16
P_SSM = 64
RMS_EPS = 1e-6
PAGE = 128
HEADS_PER_BLOCK = 8
SB_PAGES_PER_STEP = 16
FOX_PAGES_PER_STEP = 8
PROJ_ROWS = 1024
PROJ_COLS = 1024
SCAN_ROWS = 8
NEG_BIG = -1e30
MIB = 1024 * 1024


def _cparams(sem, vmem_mib):
    return pltpu.CompilerParams(dimension_semantics=sem, vmem_limit_bytes=vmem_mib * MIB)


def _split2(x):
    hi = x.astype(BF16)
    lo = (x - hi.astype(F32)).astype(BF16)
    return hi, lo


def _split3(x):
    hi = x.astype(BF16)
    r = x - hi.astype(F32)
    mid = r.astype(BF16)
    lo = (r - mid.astype(F32)).astype(BF16)
    return hi, mid, lo


def _dot(a, b):
    return jnp.dot(a, b, preferred_element_type=F32)


def _dot_nt(a, b):
    return lax.dot_general(a, b, (((1,), (1,)), ((), ())), preferred_element_type=F32)


def _neg_softplus(z):
    return -(jnp.maximum(z, 0.0) + jnp.log1p(jnp.exp(-jnp.abs(z))))


def _tri(n):
    r = lax.broadcasted_iota(jnp.int32, (n, n), 0)
    c = lax.broadcasted_iota(jnp.int32, (n, n), 1)
    return (r > c).astype(BF16)


def _norm_proj_kernel(x_ref, g_ref, w_ref, o_ref, h_scr):
    @pl.when(pl.program_id(1) == 0)
    def _():
        x = x_ref[...]
        y = x * lax.rsqrt(jnp.mean(x * x, axis=-1, keepdims=True) + RMS_EPS)
        h_scr[...] = (y * g_ref[...]).astype(BF16)

    o_ref[...] = _dot(h_scr[...], w_ref[...])


def norm_proj(x, g, w, tm, tn):
    m, d = x.shape
    n = w.shape[1] // tn * tn
    return pl.pallas_call(
        _norm_proj_kernel,
        grid=(m // tm, n // tn),
        in_specs=[
            pl.BlockSpec((tm, d), lambda i, j: (i, 0)),
            pl.BlockSpec((1, d), lambda i, j: (0, 0)),
            pl.BlockSpec((d, tn), lambda i, j: (0, j)),
        ],
        out_specs=pl.BlockSpec((tm, tn), lambda i, j: (i, j)),
        out_shape=jax.ShapeDtypeStruct((m, n), F32),
        scratch_shapes=[pltpu.VMEM((tm, d), BF16)],
        compiler_params=_cparams(("parallel", "arbitrary"), 56),
        name="norm_proj",
    )(x, g.reshape(1, d), w)


def _logf_kernel(x_ref, g_ref, w_ref, b_ref, o_ref):
    x = x_ref[...]
    y = x * lax.rsqrt(jnp.mean(x * x, axis=-1, keepdims=True) + RMS_EPS)
    h = (y * g_ref[...]).astype(BF16)
    f = _dot(h, w_ref[...]) + b_ref[...]
    o_ref[...] = _neg_softplus(-f)


def fox_logf(x, g, w_pad, b_pad, tm):
    m, d = x.shape
    n = w_pad.shape[1]
    return pl.pallas_call(
        _logf_kernel,
        grid=(m // tm,),
        in_specs=[
            pl.BlockSpec((tm, d), lambda i: (i, 0)),
            pl.BlockSpec((1, d), lambda i: (0, 0)),
            pl.BlockSpec((d, n), lambda i: (0, 0)),
            pl.BlockSpec((1, n), lambda i: (0, 0)),
        ],
        out_specs=pl.BlockSpec((tm, n), lambda i: (i, 0)),
        out_shape=jax.ShapeDtypeStruct((m, n), F32),
        compiler_params=_cparams(("parallel",), 32),
        name="fox_logf",
    )(x, g.reshape(1, d), w_pad, b_pad)


def _cumsum_kernel(x_ref, o_ref, *, blk):
    n = x_ref.shape[0]
    r = lax.broadcasted_iota(jnp.int32, (blk, blk), 0)
    c = lax.broadcasted_iota(jnp.int32, (blk, blk), 1)
    low = (c <= r).astype(BF16)

    def body(i, car):
        off = pl.multiple_of(i * blk, blk)
        hi, mid, lo = _split3(x_ref[pl.ds(off, blk), :])
        cum = _dot(low, hi) + _dot(low, mid) + _dot(low, lo) + car
        o_ref[pl.ds(off, blk), :] = cum
        return cum[blk - 1:blk, :]

    lax.fori_loop(0, n // blk, body, jnp.zeros((1, x_ref.shape[1]), F32))


def seq_cumsum(x):
    b, n, w = x.shape
    return pl.pallas_call(
        functools.partial(_cumsum_kernel, blk=128),
        grid=(b,),
        in_specs=[pl.BlockSpec((None, n, w), lambda i: (i, 0, 0))],
        out_specs=pl.BlockSpec((None, n, w), lambda i: (i, 0, 0)),
        out_shape=jax.ShapeDtypeStruct((b, n, w), F32),
        compiler_params=_cparams(("parallel",), 32),
        name="seq_cumsum",
    )(x)


def _sb_prompt_kernel(q_ref, k_ref, v_ref, o_ref, qb_scr, car_scr, *, tq, tk, hb):
    i = pl.program_id(2)
    qb_scr[...] = (q_ref[...] * (HEAD_DIM ** -0.5)).astype(BF16)
    o_ref[...] = jnp.zeros_like(o_ref)
    car_scr[...] = jnp.zeros_like(car_scr)
    t_pos = i * tq + lax.broadcasted_iota(jnp.int32, (tq, tk), 0)
    s_loc = lax.broadcasted_iota(jnp.int32, (tq, tk), 1)
    later = jnp.concatenate([_tri(tk), jnp.ones((tk, tk), BF16)], axis=1)
    nblk = (i + 1) * (tq // tk)

    def make_body(on_diagonal):
        def body(n, carry):
            off = pl.multiple_of((nblk - 1 - n) * tk, tk)
            before = (off + s_loc) < t_pos
            heads = [slice(h * HEAD_DIM, (h + 1) * HEAD_DIM) for h in range(hb)]
            zs = [_dot_nt(qb_scr[:, hc], k_ref[pl.ds(off, tk), hc].astype(BF16)) for hc in heads]
            lfs = [_neg_softplus(z) for z in zs]
            if on_diagonal:
                lfs = [jnp.where(before, lf, 0.0) for lf in lfs]
            sums = []
            for lf in lfs:
                hi, lo = _split2(lf)
                sums.append(_dot(hi, later) + _dot(lo, later))
            for h, hc in enumerate(heads):
                w = jnp.exp(zs[h] + lfs[h] + (sums[h][:, :tk] + car_scr[h]))
                if on_diagonal:
                    w = jnp.where(before, w, 0.0)
                o_ref[:, hc] += _dot(w.astype(BF16), v_ref[pl.ds(off, tk), hc].astype(BF16))
                car_scr[h] += sums[h][:, tk:]
            return carry
        return body

    lax.fori_loop(0, tq // tk, make_body(True), 0)
    lax.fori_loop(tq // tk, nblk, make_body(False), 0)


def sb_attn_prompt(proj, nb, seq, nh, tq=256, tk=256, hb=8):
    nq = seq // tq
    ng = nh // hb
    wb = hb * HEAD_DIM
    return pl.pallas_call(
        functools.partial(_sb_prompt_kernel, tq=tq, tk=tk, hb=hb),
        grid=(nb, ng, nq),
        in_specs=[
            pl.BlockSpec((tq, wb), lambda b, g, i: (b * nq + i, g)),
            pl.BlockSpec((seq, wb), lambda b, g, i: (b, ng + g)),
            pl.BlockSpec((seq, wb), lambda b, g, i: (b, 2 * ng + g)),
        ],
        out_specs=pl.BlockSpec((tq, wb), lambda b, g, i: (b * nq + i, g)),
        out_shape=jax.ShapeDtypeStruct((nb * seq, nh * HEAD_DIM), F32),
        scratch_shapes=[pltpu.VMEM((tq, wb), BF16), pltpu.VMEM((hb, tq, tk), F32)],
        compiler_params=_cparams(("parallel", "parallel", "arbitrary"), 52),
        name="sb_attn_prompt",
    )(proj, proj, proj)


def _fox_prompt_kernel(q_ref, k_ref, v_ref, cum_ref, crow_ref, o_ref, qb_scr, ct_scr, m_scr, l_scr,
                       *, tq, tk, hb):
    i = pl.program_id(2)
    reps = tk // HEAD_DIM
    qb_scr[...] = (q_ref[...] * (HEAD_DIM ** -0.5)).astype(BF16)
    o_ref[...] = jnp.zeros_like(o_ref)
    m_scr[...] = jnp.full(m_scr.shape, NEG_BIG, F32)
    l_scr[...] = jnp.zeros_like(l_scr)
    cum = cum_ref[...]
    lane = lax.broadcasted_iota(jnp.int32, cum.shape, 1)
    for h in range(hb):
        own = lane == pl.program_id(1) * hb + h
        ct_scr[h] = jnp.broadcast_to(jnp.sum(jnp.where(own, cum, 0.0), axis=1, keepdims=True),
                                     (tq, HEAD_DIM))
    t_pos = i * tq + lax.broadcasted_iota(jnp.int32, (tq, tk), 0)
    s_loc = lax.broadcasted_iota(jnp.int32, (tq, tk), 1)
    ones = jnp.ones((tk, HEAD_DIM), BF16)
    nblk = (i + 1) * (tq // tk)

    def wide(x):
        return jnp.concatenate([x] * reps, axis=1)

    def make_body(on_diagonal):
        def body(j, carry):
            off = pl.multiple_of(j * tk, tk)
            heads = [slice(h * HEAD_DIM, (h + 1) * HEAD_DIM) for h in range(hb)]
            zs = [_dot_nt(qb_scr[:, hc], k_ref[pl.ds(off, tk), hc].astype(BF16)) for hc in heads]
            ps = []
            alphas = []
            for h in range(hb):
                cs = crow_ref[h, :, pl.ds(off, tk)]
                s = zs[h] + (wide(ct_scr[h]) - cs)
                if on_diagonal:
                    s = jnp.where((off + s_loc) <= t_pos, s, NEG_BIG)
                m = m_scr[h]
                m_new = jnp.maximum(m, jnp.max(s, axis=1, keepdims=True))
                alphas.append(jnp.exp(m - m_new))
                ps.append(jnp.exp(s - wide(m_new)).astype(BF16))
                m_scr[h] = m_new
            for h, hc in enumerate(heads):
                v1 = jnp.concatenate([v_ref[pl.ds(off, tk), hc].astype(BF16), ones], axis=1)
                pv = _dot(ps[h], v1)
                l_scr[h] = alphas[h] * l_scr[h] + pv[:, HEAD_DIM:]
                o_ref[:, hc] = alphas[h] * o_ref[:, hc] + pv[:, :HEAD_DIM]
            return carry
        return body

    lax.fori_loop(0, nblk - tq // tk, make_body(False), 0)
    lax.fori_loop(nblk - tq // tk, nblk, make_body(True), 0)
    for h in range(hb):
        cols = slice(h * HEAD_DIM, (h + 1) * HEAD_DIM)
        o_ref[:, cols] = o_ref[:, cols] / l_scr[h]


def fox_attn_prompt(proj, cum, cum_row, nb, seq, nh, tq=256, tk=256, hb=8):
    nq = seq // tq
    ng = nh // hb
    wb = hb * HEAD_DIM
    return pl.pallas_call(
        functools.partial(_fox_prompt_kernel, tq=tq, tk=tk, hb=hb),
        grid=(nb, ng, nq),
        in_specs=[
            pl.BlockSpec((tq, wb), lambda b, g, i: (b * nq + i, g)),
            pl.BlockSpec((seq, wb), lambda b, g, i: (b, ng + g)),
            pl.BlockSpec((seq, wb), lambda b, g, i: (b, 2 * ng + g)),
            pl.BlockSpec((None, tq, PAGE), lambda b, g, i: (b, i, 0)),
            pl.BlockSpec((None, hb, 1, seq), lambda b, g, i: (b, g, 0, 0)),
        ],
        out_specs=pl.BlockSpec((tq, wb), lambda b, g, i: (b * nq + i, g)),
        out_shape=jax.ShapeDtypeStruct((nb * seq, nh * HEAD_DIM), F32),
        scratch_shapes=[pltpu.VMEM((tq, wb), BF16), pltpu.VMEM((hb, tq, HEAD_DIM), F32),
                        pltpu.VMEM((hb, tq, HEAD_DIM), F32), pltpu.VMEM((hb, tq, HEAD_DIM), F32)],
        compiler_params=_cparams(("parallel", "parallel", "arbitrary"), 52),
        name="fox_attn_prompt",
    )(proj, proj, proj, cum, cum_row)


GROUP_LANES = HEADS_PER_BLOCK * HEAD_DIM


def _head_select():
    lane_head = lax.broadcasted_iota(jnp.int32, (HEADS_PER_BLOCK, GROUP_LANES), 1) // HEAD_DIM
    return lane_head == lax.broadcasted_iota(jnp.int32, (HEADS_PER_BLOCK, GROUP_LANES), 0)


def _block_diag_queries(q8, nh, nq):
    sel = _head_select()
    rows = []
    for t in range(nq):
        for g in range(nh // HEADS_PER_BLOCK):
            qg = q8[t:t + 1, g * GROUP_LANES:(g + 1) * GROUP_LANES]
            rows.append(jnp.where(sel, jnp.broadcast_to(qg, (HEADS_PER_BLOCK, GROUP_LANES)), 0.0))
    rows.append(jnp.zeros((PAGE - nq * nh, GROUP_LANES), F32))
    return jnp.concatenate(rows, axis=0)


def _gather_heads(acc, nh, nq):
    sel = _head_select()
    rows = []
    for t in range(nq):
        groups = []
        for g in range(nh // HEADS_PER_BLOCK):
            r0 = t * nh + g * HEADS_PER_BLOCK
            groups.append(jnp.sum(jnp.where(sel, acc[r0:r0 + HEADS_PER_BLOCK, :], 0.0),
                                  axis=0, keepdims=True))
        rows.append(jnp.concatenate(groups, axis=1))
    rows.append(jnp.zeros((8 - nq, nh * HEAD_DIM), F32))
    return jnp.concatenate(rows, axis=0)


def _new_rows_page(rows8, nh):
    parts = [rows8[:, g * GROUP_LANES:(g + 1) * GROUP_LANES] for g in range(nh // HEADS_PER_BLOCK)]
    parts.append(jnp.zeros((PAGE - 8 * len(parts), GROUP_LANES), F32))
    return jnp.concatenate(parts, axis=0).astype(BF16)


def _load_page(ref, nh):
    n_rows = PAGE * nh // HEADS_PER_BLOCK
    parts = [ref[pl.ds(j, n_rows, stride=HEADS_PER_BLOCK), :] for j in range(HEADS_PER_BLOCK)]
    return jnp.concatenate(parts, axis=1).astype(BF16)


def _suffix_sum(x):
    n = x.shape[0]
    row = lax.broadcasted_iota(jnp.int32, x.shape, 0)
    d = 1
    while d < n:
        x = x + jnp.where(row < n - d, pltpu.roll(x, n - d, axis=0), 0.0)
        d *= 2
    return x


def _sb_decode_kernel(pt_ref, q_ref, kn_ref, vn_ref, *refs, nh, nq, pps):
    k_refs = refs[:pps]
    v_refs = refs[pps:2 * pps]
    o_ref = refs[2 * pps]
    qbd_scr, car_scr, acc_scr = refs[2 * pps + 1:]
    s = pl.program_id(1)
    nl = nq * nh

    @pl.when(s == 0)
    def _():
        qbd = _block_diag_queries(q_ref[...] * (HEAD_DIM ** -0.5), nh, nq).T
        qbd_scr[...] = qbd.astype(BF16)
        key = lax.broadcasted_iota(jnp.int32, (PAGE, PAGE), 0)
        vis = key < lax.broadcasted_iota(jnp.int32, (PAGE, PAGE), 1) // nh
        z = _dot(_new_rows_page(kn_ref[...], nh), qbd_scr[...])
        suf = _suffix_sum(jnp.where(vis, _neg_softplus(z), 0.0))
        w = jnp.where(vis, jnp.exp(z + suf), 0.0)
        car_scr[...] = suf[0:1, :]
        acc_scr[...] = _dot(w.T[:nl, :].astype(BF16), _new_rows_page(vn_ref[...], nh))

    def scores(r):
        return _dot(_load_page(k_refs[r], nh), qbd_scr[...])

    ahead = 2
    zs = [scores(r) for r in range(min(ahead, pps))]
    car = car_scr[...]
    acc = acc_scr[...]
    for r in range(pps):
        suf = _suffix_sum(_neg_softplus(zs[r]))
        w = jnp.exp(zs[r] + suf + car)
        car = car + suf[0:1, :]
        if r + ahead < pps:
            zs.append(scores(r + ahead))
        acc = acc + _dot(w.T[:nl, :].astype(BF16), _load_page(v_refs[r], nh))
    car_scr[...] = car
    acc_scr[...] = acc

    @pl.when(s == pl.num_programs(1) - 1)
    def _():
        o_ref[...] = _gather_heads(acc_scr[...], nh, nq)


def _page_specs(rows, n_pages, pps):
    specs = []
    for r in range(pps):
        def imap(b, s, pt, r=r):
            return (pt[b, n_pages - 1 - (s * pps + r)], 0, 0)
        specs.append(pl.BlockSpec((None, rows, HEAD_DIM), imap))
    return specs


def sb_attn_decode(page_table, q8, kn8, vn8, pool_k, pool_v, nh, nq, pps):
    assert nh == HEADS_PER_BLOCK
    nb, n_pages = page_table.shape
    w = nh * HEAD_DIM
    row_spec = pl.BlockSpec((None, 8, w), lambda b, s, pt: (b, 0, 0))
    grid_spec = pltpu.PrefetchScalarGridSpec(
        num_scalar_prefetch=1,
        grid=(nb, n_pages // pps),
        in_specs=[row_spec, row_spec, row_spec]
        + _page_specs(PAGE * nh, n_pages, pps) + _page_specs(PAGE * nh, n_pages, pps),
        out_specs=row_spec,
        scratch_shapes=[
            pltpu.VMEM((GROUP_LANES, PAGE), BF16),
            pltpu.VMEM((1, PAGE), F32),
            pltpu.VMEM((nq * nh, GROUP_LANES), F32),
        ],
    )
    return pl.pallas_call(
        functools.partial(_sb_decode_kernel, nh=nh, nq=nq, pps=pps),
        grid_spec=grid_spec,
        out_shape=jax.ShapeDtypeStruct((nb, 8, w), F32),
        compiler_params=_cparams(("arbitrary", "arbitrary"), 56),
        name="sb_attn_decode",
    )(page_table, q8, kn8, vn8, *([pool_k] * pps), *([pool_v] * pps))


def _tile_lanes(x16, nh, nq):
    out = x16
    for t in range(1, nq):
        out = out + pltpu.roll(x16, t * nh, axis=1)
    return out


def _fox_decode_kernel(pt_ref, q_ref, kn_ref, vn_ref, lfn_ref, *refs, nh, nq, pps):
    ng = nh // HEADS_PER_BLOCK
    k_refs = refs[:pps]
    v_refs = refs[pps:2 * pps]
    lf_refs = refs[2 * pps:3 * pps]
    o_ref = refs[3 * pps]
    qbd_scr, lfpad_scr, bias_scr, car_scr, pre_scr, m_scr, l_scr, acc_scr = refs[3 * pps + 1:]
    s = pl.program_id(1)
    nl = nq * nh
    n_rows = PAGE * ng
    row_group = lax.broadcasted_iota(jnp.int32, (n_rows, PAGE), 0) % ng
    lane_group = (lax.broadcasted_iota(jnp.int32, (n_rows, PAGE), 1) % nh) // HEADS_PER_BLOCK
    valid = row_group == lane_group

    def to_col(row):
        return jnp.broadcast_to(row, (PAGE, PAGE)).T[:nl, 0:1]

    def to_rows(p):
        blocks = [p[i * PAGE:(i + 1) * PAGE, :].T for i in range(p.shape[0] // PAGE)]
        return jnp.concatenate(blocks, axis=1)[:nl, :].astype(BF16)

    @pl.when(s == 0)
    def _():
        qbd = _block_diag_queries(q_ref[...] * (HEAD_DIM ** -0.5), nh, nq).T
        qbd_scr[...] = qbd.astype(BF16)
        lfpad_scr[...] = jnp.zeros((PAGE, PAGE), F32)
        lfn = lfn_ref[...]
        key8 = lax.broadcasted_iota(jnp.int32, (8, PAGE), 0)
        qry8 = lax.broadcasted_iota(jnp.int32, (8, PAGE), 1) // nh
        pre_q = jnp.sum(jnp.where(key8 <= qry8, lfn, 0.0), axis=0, keepdims=True)
        run = jnp.zeros((1, PAGE), F32)
        pre_rows = []
        for j in range(8):
            run = run + lfn[j:j + 1, :]
            pre_rows.append(run)
        pre_k = jnp.concatenate(pre_rows, axis=0)
        nr = 8 * ng
        rown = lax.broadcasted_iota(jnp.int32, (nr, PAGE), 0)
        lanen = lax.broadcasted_iota(jnp.int32, (nr, PAGE), 1)
        seen = ((rown // 8) == (lanen % nh) // HEADS_PER_BLOCK) & ((rown % 8) <= lanen // nh)
        z = _dot(_new_rows_page(kn_ref[...], nh), qbd_scr[...])[:nr, :]
        sc = jnp.where(seen, z + jnp.concatenate([pre_q - pre_k] * ng, axis=0), NEG_BIG)
        m = jnp.max(sc, axis=0, keepdims=True)
        p = jnp.exp(sc - m)
        pt = to_rows(jnp.concatenate([p, jnp.zeros((PAGE - nr, PAGE), F32)], axis=0))
        acc_scr[...] = _dot(pt, _new_rows_page(vn_ref[...], nh))
        l_scr[...] = jnp.sum(p, axis=0, keepdims=True)
        m_scr[...] = m
        pre_scr[...] = pre_q
        car_scr[...] = jnp.zeros((1, PAGE), F32)

    zs = [_dot(_load_page(k_refs[r], nh), qbd_scr[...]) for r in range(pps)]
    car = car_scr[...]
    pre_q = pre_scr[...]
    scores = []
    for r in range(pps):
        lfpad_scr[:, 0:nh] = lf_refs[r][...]
        lf = _tile_lanes(lfpad_scr[...], nh, nq)
        suf = _suffix_sum(lf)
        bias = (suf - lf) + car + pre_q
        for g in range(ng):
            bias_scr[pl.ds(g, PAGE, stride=ng), :] = bias
        scores.append(jnp.where(valid, zs[r] + bias_scr[...], NEG_BIG))
        car = car + suf[0:1, :]
    car_scr[...] = car

    m_old = m_scr[...]
    m_new = m_old
    for sc in scores:
        m_new = jnp.maximum(m_new, jnp.max(sc, axis=0, keepdims=True))
    alpha = jnp.exp(m_old - m_new)
    l = alpha * l_scr[...]
    acc = acc_scr[...] * to_col(alpha)
    for r in range(pps):
        p = jnp.exp(scores[r] - m_new)
        l = l + jnp.sum(p, axis=0, keepdims=True)
        acc = acc + _dot(to_rows(p), _load_page(v_refs[r], nh))
    acc_scr[...] = acc
    l_scr[...] = l
    m_scr[...] = m_new

    @pl.when(s == pl.num_programs(1) - 1)
    def _():
        o_ref[...] = _gather_heads(acc_scr[...] / to_col(l_scr[...]), nh, nq)


def fox_attn_decode(page_table, q8, kn8, vn8, lfn8, pool_k, pool_v, pool_lf, nh, nq, pps):
    nb, n_pages = page_table.shape
    w = nh * HEAD_DIM
    ng = nh // HEADS_PER_BLOCK
    row_spec = pl.BlockSpec((None, 8, w), lambda b, s, pt: (b, 0, 0))
    lf_specs = []
    for r in range(pps):
        def imap(b, s, pt, r=r):
            return (pt[b, n_pages - 1 - (s * pps + r)], 0, 0)
        lf_specs.append(pl.BlockSpec((None, PAGE, nh), imap))
    grid_spec = pltpu.PrefetchScalarGridSpec(
        num_scalar_prefetch=1,
        grid=(nb, n_pages // pps),
        in_specs=[row_spec, row_spec, row_spec,
                  pl.BlockSpec((None, 8, PAGE), lambda b, s, pt: (b, 0, 0))]
        + _page_specs(PAGE * nh, n_pages, pps) + _page_specs(PAGE * nh, n_pages, pps) + lf_specs,
        out_specs=row_spec,
        scratch_shapes=[
            pltpu.VMEM((GROUP_LANES, PAGE), BF16),
            pltpu.VMEM((PAGE, PAGE), F32),
            pltpu.VMEM((PAGE * ng, PAGE), F32),
            pltpu.VMEM((1, PAGE), F32),
            pltpu.VMEM((1, PAGE), F32),
            pltpu.VMEM((1, PAGE), F32),
            pltpu.VMEM((1, PAGE), F32),
            pltpu.VMEM((nq * nh, GROUP_LANES), F32),
        ],
    )
    return pl.pallas_call(
        functools.partial(_fox_decode_kernel, nh=nh, nq=nq, pps=pps),
        grid_spec=grid_spec,
        out_shape=jax.ShapeDtypeStruct((nb, 8, w), F32),
        compiler_params=_cparams(("arbitrary", "arbitrary"), 56),
        name="fox_attn_decode",
    )(page_table, q8, kn8, vn8, lfn8, *([pool_k] * pps), *([pool_v] * pps), *([pool_lf] * pps))


def _ssm_prep_kernel(are_ref, aim_ref, ldt_ref, bre_ref, bim_ref,
                     pwr_ref, pwi_ref, bbr_ref, bbi_ref):
    ar = are_ref[...]
    ai = aim_ref[...]
    dt = jnp.exp(ldt_ref[...])
    mag = jnp.exp(ar * dt)
    abr = mag * jnp.cos(ai * dt)
    abi = mag * jnp.sin(ai * dt)
    den = ar * ar + ai * ai
    nr = abr - 1.0
    ni = abi
    cr = (nr * ar + ni * ai) / den
    ci = (ni * ar - nr * ai) / den
    br = bre_ref[...]
    bi = bim_ref[...]
    bbr_ref[...] = cr * br - ci * bi
    bbi_ref[...] = cr * bi + ci * br
    pr, pi = abr, abi
    pwr_ref[0] = pr
    pwi_ref[0] = pi
    for k in range(1, SCAN_ROWS):
        pr, pi = pr * abr - pi * abi, pr * abi + pi * abr
        pwr_ref[k] = pr
        pwi_ref[k] = pi


def ssm_prep(a_re, a_im, log_dt, b_re_t, b_im_t):
    g, p = a_re.shape
    c = b_re_t.shape[1]
    return pl.pallas_call(
        _ssm_prep_kernel,
        out_shape=(jax.ShapeDtypeStruct((SCAN_ROWS, g, 1, p), F32),
                   jax.ShapeDtypeStruct((SCAN_ROWS, g, 1, p), F32),
                   jax.ShapeDtypeStruct((g, c, p), F32), jax.ShapeDtypeStruct((g, c, p), F32)),
        name="ssm_prep",
    )(a_re.reshape(g, 1, p), a_im.reshape(g, 1, p), log_dt.reshape(g, 1, 1), b_re_t, b_im_t)


def _ssm_prompt_kernel(u_ref, wbr_ref, wbi_ref, wcr_ref, wci_ref, d_ref, pwr_ref, pwi_ref,
                       y_ref, hr_ref, hi_ref, xr_scr, xi_scr, sr_scr, si_scr, *, tt):
    i = pl.program_id(2)
    lanes = xr_scr.shape[1]

    @pl.when(i == 0)
    def _():
        sr_scr[...] = jnp.zeros_like(sr_scr)
        si_scr[...] = jnp.zeros_like(si_scr)

    u = u_ref[...]
    ub = u.astype(BF16)
    xr_scr[...] = _dot(ub, wbr_ref[...])
    xi_scr[...] = _dot(ub, wbi_ref[...])

    pw_r = pwr_ref[...]
    pw_i = pwi_ref[...]
    row = lax.broadcasted_iota(jnp.int32, (SCAN_ROWS, lanes), 0)

    def tile(r, carry):
        h_r, h_i = carry
        off = pl.multiple_of(r * SCAN_ROWS, SCAN_ROWS)
        x_r = xr_scr[pl.ds(off, SCAN_ROWS), :]
        x_i = xi_scr[pl.ds(off, SCAN_ROWS), :]
        for sh in (1, 2, 4):
            a_r = pw_r[sh - 1:sh, :]
            a_i = pw_i[sh - 1:sh, :]
            p_r = jnp.where(row >= sh, pltpu.roll(x_r, sh, axis=0), 0.0)
            p_i = jnp.where(row >= sh, pltpu.roll(x_i, sh, axis=0), 0.0)
            x_r, x_i = x_r + (a_r * p_r - a_i * p_i), x_i + (a_r * p_i + a_i * p_r)
        x_r, x_i = x_r + (pw_r * h_r - pw_i * h_i), x_i + (pw_r * h_i + pw_i * h_r)
        xr_scr[pl.ds(off, SCAN_ROWS), :] = x_r
        xi_scr[pl.ds(off, SCAN_ROWS), :] = x_i
        return x_r[SCAN_ROWS - 1:SCAN_ROWS, :], x_i[SCAN_ROWS - 1:SCAN_ROWS, :]

    h_r, h_i = lax.fori_loop(0, tt // SCAN_ROWS, tile, (sr_scr[...], si_scr[...]))
    sr_scr[...] = h_r
    si_scr[...] = h_i
    hr_ref[...] = h_r
    hi_ref[...] = h_i
    y_ref[...] = (_dot(xr_scr[...].astype(BF16), wcr_ref[...])
                  - _dot(xi_scr[...].astype(BF16), wci_ref[...]) + d_ref[...] * u)


def ssm_prompt(proj, u_col, nb, seq, wbr, wbi, wcr, wci, d, pw_r, pw_i, tt=512):
    nchunk, cw, lanes = wbr.shape
    nt = seq // tt
    ucb = u_col * (nchunk)
    return pl.pallas_call(
        functools.partial(_ssm_prompt_kernel, tt=tt),
        grid=(nb, nchunk, nt),
        in_specs=[
            pl.BlockSpec((tt, cw), lambda b, j, i: (b * nt + i, ucb + j)),
            pl.BlockSpec((None, cw, lanes), lambda b, j, i: (j, 0, 0)),
            pl.BlockSpec((None, cw, lanes), lambda b, j, i: (j, 0, 0)),
            pl.BlockSpec((None, lanes, cw), lambda b, j, i: (j, 0, 0)),
            pl.BlockSpec((None, lanes, cw), lambda b, j, i: (j, 0, 0)),
            pl.BlockSpec((1, cw), lambda b, j, i: (0, j)),
            pl.BlockSpec((SCAN_ROWS, lanes), lambda b, j, i: (0, j)),
            pl.BlockSpec((SCAN_ROWS, lanes), lambda b, j, i: (0, j)),
        ],
        out_specs=(
            pl.BlockSpec((tt, cw), lambda b, j, i: (b * nt + i, j)),
            pl.BlockSpec((None, 1, lanes), lambda b, j, i: (b, 0, j)),
            pl.BlockSpec((None, 1, lanes), lambda b, j, i: (b, 0, j)),
        ),
        out_shape=(jax.ShapeDtypeStruct((nb * seq, nchunk * cw), F32),
                   jax.ShapeDtypeStruct((nb, 1, nchunk * lanes), F32),
                   jax.ShapeDtypeStruct((nb, 1, nchunk * lanes), F32)),
        scratch_shapes=[pltpu.VMEM((tt, lanes), F32), pltpu.VMEM((tt, lanes), F32),
                        pltpu.VMEM((1, lanes), F32), pltpu.VMEM((1, lanes), F32)],
        compiler_params=_cparams(("parallel", "parallel", "arbitrary"), 32),
        name="ssm_prompt",
    )(proj, wbr, wbi, wcr, wci, d, pw_r, pw_i)


def _ssm_sample_kernel(u_ref, h0r_ref, h0i_ref, wbr_ref, wbi_ref, wcr_ref, wci_ref, d_ref,
                       pwr_ref, pwi_ref, y_ref, hr_ref, hi_ref, *, steps):
    a_r = pwr_ref[0:1, :]
    a_i = pwi_ref[0:1, :]
    h_r = h0r_ref[...]
    h_i = h0i_ref[...]
    for t in range(steps):
        u = u_ref[t]
        ub = u.astype(BF16)
        h_r, h_i = (a_r * h_r - a_i * h_i + _dot(ub, wbr_ref[...]),
                    a_r * h_i + a_i * h_r + _dot(ub, wbi_ref[...]))
        y_ref[t] = (_dot(h_r.astype(BF16), wcr_ref[...]) - _dot(h_i.astype(BF16), wci_ref[...])
                    + d_ref[...] * u)
    hr_ref[...] = h_r
    hi_ref[...] = h_i


def ssm_sample(u_t, h0_r, h0_i, wbr, wbi, wcr, wci, d, pw_r, pw_i):
    steps, nb, _ = u_t.shape
    nchunk, cw, lanes = wbr.shape
    return pl.pallas_call(
        functools.partial(_ssm_sample_kernel, steps=steps),
        grid=(nchunk,),
        in_specs=[
            pl.BlockSpec((steps, nb, cw), lambda j: (0, 0, j)),
            pl.BlockSpec((nb, lanes), lambda j: (0, j)),
            pl.BlockSpec((nb, lanes), lambda j: (0, j)),
            pl.BlockSpec((None, cw, lanes), lambda j: (j, 0, 0)),
            pl.BlockSpec((None, cw, lanes), lambda j: (j, 0, 0)),
            pl.BlockSpec((None, lanes, cw), lambda j: (j, 0, 0)),
            pl.BlockSpec((None, lanes, cw), lambda j: (j, 0, 0)),
            pl.BlockSpec((1, cw), lambda j: (0, j)),
            pl.BlockSpec((SCAN_ROWS, lanes), lambda j: (0, j)),
            pl.BlockSpec((SCAN_ROWS, lanes), lambda j: (0, j)),
        ],
        out_specs=(
            pl.BlockSpec((steps, nb, cw), lambda j: (0, 0, j)),
            pl.BlockSpec((nb, lanes), lambda j: (0, j)),
            pl.BlockSpec((nb, lanes), lambda j: (0, j)),
        ),
        out_shape=(jax.ShapeDtypeStruct((steps, nb, nchunk * cw), F32),
                   jax.ShapeDtypeStruct((nb, nchunk * lanes), F32),
                   jax.ShapeDtypeStruct((nb, nchunk * lanes), F32)),
        compiler_params=_cparams(("parallel",), 32),
        name="ssm_sample",
    )(u_t, h0_r, h0_i, wbr, wbi, wcr, wci, d, pw_r, pw_i)


def _even_out_kernel(oa_ref, ga_ref, yb_ref, gb_ref, x_ref, wg_ref, bg_ref, wo_ref, o_ref):
    half = oa_ref.shape[1]
    ya = oa_ref[...] * jax.nn.silu(ga_ref[...])
    yb = jax.nn.gelu(yb_ref[...])
    yb = yb * jax.nn.sigmoid(_dot(yb.astype(BF16), wg_ref[...]) + bg_ref[...])
    yb = yb * jax.nn.silu(gb_ref[...])
    o_ref[...] = (x_ref[...] + _dot(ya.astype(BF16), wo_ref[0:half, :])
                  + _dot(yb.astype(BF16), wo_ref[half:, :]))


def even_out(o_a, y_b, proj, x, w_glu, b_glu, w_out, tm=256):
    m, half = o_a.shape
    tm = min(tm, m)
    d = x.shape[1]
    return pl.pallas_call(
        _even_out_kernel,
        grid=(m // tm,),
        in_specs=[
            pl.BlockSpec((tm, half), lambda i: (i, 0)),
            pl.BlockSpec((tm, half), lambda i: (i, 3)),
            pl.BlockSpec((tm, half), lambda i: (i, 0)),
            pl.BlockSpec((tm, half), lambda i: (i, 5)),
            pl.BlockSpec((tm, d), lambda i: (i, 0)),
            pl.BlockSpec((half, half), lambda i: (0, 0)),
            pl.BlockSpec((1, half), lambda i: (0, 0)),
            pl.BlockSpec((2 * half, d), lambda i: (0, 0)),
        ],
        out_specs=pl.BlockSpec((tm, d), lambda i: (i, 0)),
        out_shape=jax.ShapeDtypeStruct((m, d), F32),
        compiler_params=_cparams(("parallel",), 48),
        name="even_out",
    )(o_a, proj, y_b, proj, x, w_glu, b_glu, w_out)


def _odd_out_kernel(o_ref_in, g_ref, x_ref, wo_ref, gf_ref, y_ref):
    y = o_ref_in[...] * jax.nn.silu(g_ref[...])
    r = x_ref[...] + _dot(y.astype(BF16), wo_ref[...])
    n = r * lax.rsqrt(jnp.mean(r * r, axis=-1, keepdims=True) + RMS_EPS)
    y_ref[...] = n * gf_ref[...]


def odd_out(o, proj, x, w_out, g_final, tm=256):
    m, d = x.shape
    tm = min(tm, m)
    w = o.shape[1]
    return pl.pallas_call(
        _odd_out_kernel,
        grid=(m // tm,),
        in_specs=[
            pl.BlockSpec((tm, w), lambda i: (i, 0)),
            pl.BlockSpec((tm, w), lambda i: (i, 3)),
            pl.BlockSpec((tm, d), lambda i: (i, 0)),
            pl.BlockSpec((w, d), lambda i: (0, 0)),
            pl.BlockSpec((1, d), lambda i: (0, 0)),
        ],
        out_specs=pl.BlockSpec((tm, d), lambda i: (i, 0)),
        out_shape=jax.ShapeDtypeStruct((m, d), F32),
        compiler_params=_cparams(("parallel",), 48),
        name="odd_out",
    )(o, proj, x, w_out, g_final.reshape(1, d))


def _pad_rows8(x):
    return jnp.pad(x, ((0, 0), (0, 8 - x.shape[1]), (0, 0)))


def _block_diag(x):
    nc, pc, a, b = x.shape
    eye = jnp.eye(pc, dtype=x.dtype)
    return jnp.einsum("jgab,gh->jgahb", x, eye).reshape(nc, pc * a, pc * b)


def kernel(x_prompt, x_sample, cache_sb_k, cache_sb_v, state_ssm_re, state_ssm_im, cache_fox_k, cache_fox_v, cache_fox_logf, page_table, norm_g, w_in_even, ssm_a_re, ssm_a_im, ssm_log_dt, ssm_b_re, ssm_b_im, ssm_c_re, ssm_c_im, ssm_d, w_glu, b_glu, w_out_even, w_in_odd, b_f, w_out_odd, norm_f_g):
    nbp, seq, d_model = x_prompt.shape
    nbs, dec = x_sample.shape[:2]
    mp = nbp * seq
    ms = nbs * dec
    w_sb = d_model // 2
    h_sb = w_sb // HEAD_DIM
    w_ssm = d_model - w_sb
    g_ssm = w_ssm // SSM_GROUP
    h_fox = d_model // HEAD_DIM
    n_pool = cache_sb_k.shape[1]
    groups_per_chunk = HEAD_DIM // SSM_GROUP
    n_chunks = g_ssm // groups_per_chunk

    xp = x_prompt.reshape(mp, d_model)
    xs = x_sample.reshape(ms, d_model)

    w_in0 = w_in_even[0].astype(BF16)
    proj0_p = norm_proj(xp, norm_g[0], w_in0, tm=PROJ_ROWS, tn=PROJ_COLS)
    proj0_s = norm_proj(xs, norm_g[0], w_in0, tm=ms, tn=PROJ_COLS)
    k0_s = proj0_s[:, w_sb:2 * w_sb]
    v0_s = proj0_s[:, 2 * w_sb:3 * w_sb]

    oa_p = sb_attn_prompt(proj0_p, nbp, seq, h_sb)
    q8 = _pad_rows8(proj0_s[:, :w_sb].reshape(nbs, dec, w_sb))
    kn8 = _pad_rows8(k0_s.reshape(nbs, dec, w_sb))
    vn8 = _pad_rows8(v0_s.reshape(nbs, dec, w_sb))
    oa_s = sb_attn_decode(page_table, q8, kn8, vn8,
                          cache_sb_k.reshape(n_pool, PAGE * h_sb, HEAD_DIM),
                          cache_sb_v.reshape(n_pool, PAGE * h_sb, HEAD_DIM),
                          h_sb, dec, SB_PAGES_PER_STEP)[:, :dec].reshape(ms, w_sb)

    b_re_t = jnp.swapaxes(ssm_b_re[0], 1, 2)
    b_im_t = jnp.swapaxes(ssm_b_im[0], 1, 2)
    pw_r, pw_i, bb_r, bb_i = ssm_prep(ssm_a_re[0], ssm_a_im[0], ssm_log_dt[0], b_re_t, b_im_t)
    n_state = g_ssm * P_SSM
    pw_r = pw_r.reshape(SCAN_ROWS, n_state)
    pw_i = pw_i.reshape(SCAN_ROWS, n_state)
    wbr = _block_diag(bb_r.reshape(n_chunks, groups_per_chunk, SSM_GROUP, P_SSM)).astype(BF16)
    wbi = _block_diag(bb_i.reshape(n_chunks, groups_per_chunk, SSM_GROUP, P_SSM)).astype(BF16)
    c_re_t = jnp.swapaxes(ssm_c_re[0], 1, 2)
    c_im_t = jnp.swapaxes(ssm_c_im[0], 1, 2)
    wcr = _block_diag(c_re_t.reshape(n_chunks, groups_per_chunk, P_SSM, SSM_GROUP)).astype(BF16)
    wci = _block_diag(c_im_t.reshape(n_chunks, groups_per_chunk, P_SSM, SSM_GROUP)).astype(BF16)
    d_row = ssm_d[0].reshape(1, w_ssm)

    yb_p, hr_p, hi_p = ssm_prompt(proj0_p, 4, nbp, seq, wbr, wbi, wcr, wci, d_row, pw_r, pw_i)
    u_s = jnp.swapaxes(proj0_s[:, 4 * w_sb:4 * w_sb + w_ssm].reshape(nbs, dec, w_ssm), 0, 1)
    yb_s, hr_s, hi_s = ssm_sample(u_s, state_ssm_re[0].reshape(nbs, n_state),
                                  state_ssm_im[0].reshape(nbs, n_state),
                                  wbr, wbi, wcr, wci, d_row, pw_r, pw_i)
    yb_s = jnp.swapaxes(yb_s, 0, 1).reshape(ms, w_ssm)

    w_glu0 = w_glu[0].astype(BF16)
    b_glu0 = b_glu[0].reshape(1, w_ssm)
    w_out0 = w_out_even[0].astype(BF16)
    x1_p = even_out(oa_p, yb_p, proj0_p, xp, w_glu0, b_glu0, w_out0)
    x1_s = even_out(oa_s, yb_s, proj0_s, xs, w_glu0, b_glu0, w_out0)

    w_main = w_in_odd[0].astype(BF16)
    w_f = jnp.pad(w_main[:, 4 * d_model:], ((0, 0), (0, PAGE - h_fox)))
    b_pad = jnp.pad(b_f[0], (0, PAGE - h_fox)).reshape(1, PAGE)
    proj1_p = norm_proj(x1_p, norm_g[1], w_main, tm=PROJ_ROWS, tn=PROJ_COLS)
    proj1_s = norm_proj(x1_s, norm_g[1], w_main, tm=ms, tn=PROJ_COLS)
    logf_p = fox_logf(x1_p, norm_g[1], w_f, b_pad, tm=512).reshape(nbp, seq, PAGE)
    logf_s = fox_logf(x1_s, norm_g[1], w_f, b_pad, tm=ms)[:, :h_fox].reshape(nbs, dec, h_fox)

    cum = seq_cumsum(logf_p)
    cum_row = jnp.transpose(cum[:, :, :h_fox], (0, 2, 1))[:, :, None, :]
    o_p = fox_attn_prompt(proj1_p, cum, cum_row, nbp, seq, h_fox)

    k1_s = proj1_s[:, d_model:2 * d_model]
    v1_s = proj1_s[:, 2 * d_model:3 * d_model]
    q8 = _pad_rows8(proj1_s[:, :d_model].reshape(nbs, dec, d_model))
    kn8 = _pad_rows8(k1_s.reshape(nbs, dec, d_model))
    vn8 = _pad_rows8(v1_s.reshape(nbs, dec, d_model))
    lfn8 = _pad_rows8(jnp.pad(jnp.tile(logf_s, (1, 1, dec)), ((0, 0), (0, 0), (0, PAGE - dec * h_fox))))
    o_s = fox_attn_decode(page_table, q8, kn8, vn8, lfn8,
                          cache_fox_k.reshape(n_pool, PAGE * h_fox, HEAD_DIM),
                          cache_fox_v.reshape(n_pool, PAGE * h_fox, HEAD_DIM),
                          cache_fox_logf.reshape(n_pool, PAGE, h_fox),
                          h_fox, dec, FOX_PAGES_PER_STEP)[:, :dec].reshape(ms, d_model)

    w_out1 = w_out_odd[0].astype(BF16)
    y_prompt = odd_out(o_p, proj1_p, x1_p, w_out1, norm_f_g).reshape(nbp, seq, d_model)
    y_sample = odd_out(o_s, proj1_s, x1_s, w_out1, norm_f_g).reshape(nbs, dec, d_model)

    sb_k_prompt = proj0_p[:, w_sb:2 * w_sb].reshape(1, nbp, seq, h_sb, HEAD_DIM)
    sb_v_prompt = proj0_p[:, 2 * w_sb:3 * w_sb].reshape(1, nbp, seq, h_sb, HEAD_DIM)
    sb_k_sample = k0_s.reshape(1, nbs, dec, h_sb, HEAD_DIM)
    sb_v_sample = v0_s.reshape(1, nbs, dec, h_sb, HEAD_DIM)
    ssm_re_prompt = hr_p.reshape(1, nbp, g_ssm, P_SSM)
    ssm_im_prompt = hi_p.reshape(1, nbp, g_ssm, P_SSM)
    ssm_re_sample = hr_s.reshape(1, nbs, g_ssm, P_SSM)
    ssm_im_sample = hi_s.reshape(1, nbs, g_ssm, P_SSM)
    fox_k_prompt = proj1_p[:, d_model:2 * d_model].reshape(1, nbp, seq, h_fox, HEAD_DIM)
    fox_v_prompt = proj1_p[:, 2 * d_model:3 * d_model].reshape(1, nbp, seq, h_fox, HEAD_DIM)
    fox_logf_prompt = logf_p[:, :, :h_fox].reshape(1, nbp, seq, h_fox)
    fox_k_sample = k1_s.reshape(1, nbs, dec, h_fox, HEAD_DIM)
    fox_v_sample = v1_s.reshape(1, nbs, dec, h_fox, HEAD_DIM)
    fox_logf_sample = logf_s.reshape(1, nbs, dec, h_fox)
    return (y_prompt, y_sample, sb_k_prompt, sb_v_prompt, sb_k_sample, sb_v_sample,
            ssm_re_prompt, ssm_im_prompt, ssm_re_sample, ssm_im_sample,
            fox_k_prompt, fox_v_prompt, fox_logf_prompt, fox_k_sample, fox_v_sample, fox_logf_sample)
```
